```python
import jax, jax.numpy as jnp
from jax import lax
import numpy as np

D_MODEL = 1024
BATCH = 1
SEQ = 16384
DEPTH = 1
DEC_BATCH = 128
DEC_SEQ = 1
PAST_LEN = 16384
PAGE_SIZE = 128

HEAD_DIM = 64
ATTN_HEADS = 8
KV_HEADS = 2
GQA_GROUP = ATTN_HEADS // KV_HEADS
WINDOW = 128
ROPE_THETA = 10000.0
GM_HEADS = 8
GM_HEAD_DIM = 64
CHUNK = 128
ATTN_WIDTH = ATTN_HEADS * HEAD_DIM
KV_WIDTH = KV_HEADS * HEAD_DIM
GM_WIDTH = GM_HEADS * GM_HEAD_DIM
D_MIX = ATTN_WIDTH + GM_WIDTH
D_IN = ATTN_WIDTH + 2 * KV_WIDTH + 2 * GM_WIDTH
SPLITS = (ATTN_WIDTH, ATTN_WIDTH + KV_WIDTH, ATTN_WIDTH + 2 * KV_WIDTH, ATTN_WIDTH + 2 * KV_WIDTH + GM_WIDTH)
PEER_HEADS = 8
N_KEYS = 128
N_EXPERTS = N_KEYS * N_KEYS
D_KEY = 256
HALF_KEY = D_KEY // 2
TOPK = 16
PEER_BLOCK = 128
EPS = 1e-6
NEG_INF = -1e30

kernel_name = "hymba_swa_sink_gmlp_peer_step"


def rms_norm(x, w):
    xf = x.astype(jnp.float32)
    y = xf * lax.rsqrt(jnp.mean(xf * xf, axis=-1, keepdims=True) + EPS)
    return (y * w.astype(jnp.float32)).astype(x.dtype)


def rope(x, pos):
    half = HEAD_DIM // 2
    inv = ROPE_THETA ** (-jnp.arange(half, dtype=jnp.float32) / half)
    ang = pos.astype(jnp.float32)[:, None] * inv[None, :]
    cos = jnp.cos(ang)[:, None, :]
    sin = jnp.sin(ang)[:, None, :]
    xf = x.astype(jnp.float32)
    x1, x2 = xf[..., :half], xf[..., half:]
    return jnp.concatenate([x1 * cos - x2 * sin, x2 * cos + x1 * sin], axis=-1).astype(x.dtype)


def mixer_inputs(xn, pos, w_in, q_norm_w, k_norm_w, gm_v_norm_w):
    B, S = xn.shape[:2]
    q, k, v, u, gv = jnp.split(xn @ w_in, SPLITS, axis=-1)
    q = rope(rms_norm(q.reshape(B, S, ATTN_HEADS, HEAD_DIM), q_norm_w), pos)
    k = rope(rms_norm(k.reshape(B, S, KV_HEADS, HEAD_DIM), k_norm_w), pos)
    v = v.reshape(B, S, KV_HEADS, HEAD_DIM)
    u = jax.nn.gelu(u, approximate=False)
    gv = rms_norm(jax.nn.gelu(gv, approximate=False).reshape(B, S, GM_HEADS, GM_HEAD_DIM), gm_v_norm_w)
    return q, k, v, u, gv


def sink_attention(q, k, v, sinks, mask):
    B, N, Q = q.shape[:3]
    qg = q.reshape(B, N, Q, KV_HEADS, GQA_GROUP, HEAD_DIM).astype(jnp.float32)
    s = jnp.einsum('bnqkgd,bnlkd->bnkgql', qg, k.astype(jnp.float32)) * (HEAD_DIM ** -0.5)
    s = jnp.where(mask[None, :, None, None], s, NEG_INF)
    sink = sinks.astype(jnp.float32).reshape(KV_HEADS, GQA_GROUP)[None, None, :, :, None, None]
    m = jnp.maximum(jnp.max(s, axis=-1, keepdims=True), sink)
    p = jnp.exp(s - m)
    denom = jnp.sum(p, axis=-1, keepdims=True) + jnp.exp(sink - m)
    o = jnp.einsum('bnkgql,bnlkd->bnqkgd', p / denom, v.astype(jnp.float32))
    return o.reshape(B, N, Q, ATTN_WIDTH).astype(q.dtype)


def spatial_gate(u, gv_chunks, w_spatial, b_spatial):
    C = gv_chunks.shape[2]
    w = jnp.tril(w_spatial[:, :C, :C])
    mixed = jnp.einsum('hts,bnshd->bnthd', w, gv_chunks) + b_spatial[:, :C].T[:, :, None]
    return u * mixed.reshape(u.shape)


def merge_heads(attn, gm, out_norm_w, w_out):
    a = rms_norm(attn, out_norm_w[:ATTN_WIDTH])
    g = rms_norm(gm, out_norm_w[ATTN_WIDTH:])
    return jnp.concatenate([a, g], axis=-1) @ w_out


def peer_tokens(xn, w_query, sub_keys, expert_u, expert_v):
    n = xn.shape[0]
    q = (xn @ w_query).reshape(n, PEER_HEADS, 2, HALF_KEY).astype(jnp.float32)
    s = jnp.einsum('nhpc,hpkc->nhpk', q, sub_keys.astype(jnp.float32))
    s_top, i_top = lax.top_k(s, TOPK)
    cand = (s_top[:, :, 0, :, None] + s_top[:, :, 1, None, :]).reshape(n, PEER_HEADS, TOPK * TOPK)
    best, idx = lax.top_k(cand, TOPK)
    i1 = jnp.take_along_axis(i_top[:, :, 0], idx // TOPK, axis=-1)
    i2 = jnp.take_along_axis(i_top[:, :, 1], idx % TOPK, axis=-1)
    expert = i1 * N_KEYS + i2
    g = jax.nn.softmax(best, axis=-1)
    act = jax.nn.gelu(jnp.einsum('nhkd,nd->nhk', expert_u[expert], xn).astype(jnp.float32), approximate=False)
    return jnp.einsum('nhk,nhkd->nd', (g * act).astype(xn.dtype), expert_v[expert])


def peer(xn, w_query, sub_keys, expert_u, expert_v):
    shp = xn.shape
    flat = xn.reshape(-1, D_MODEL)
    n = flat.shape[0]
    nb = -(-n // PEER_BLOCK)
    flat = jnp.pad(flat, ((0, nb * PEER_BLOCK - n), (0, 0)))
    out = lax.map(lambda blk: peer_tokens(blk, w_query, sub_keys, expert_u, expert_v),
                  flat.reshape(nb, PEER_BLOCK, D_MODEL))
    return out.reshape(-1, D_MODEL)[:n].reshape(shp)


def setup_inputs(seed: int = 0) -> dict:
    key = jax.random.key(seed)
    ks = jax.random.split(key, 20)
    f32 = jnp.float32
    nrm = lambda k, shape, scale: (scale * jax.random.normal(k, shape)).astype(f32)
    gain = lambda k, shape: (1.0 + 0.01 * jax.random.normal(k, shape)).astype(f32)
    return {
        "x_prompt": nrm(ks[0], (BATCH, SEQ, D_MODEL), 1.0),
        "x_sample": nrm(ks[1], (DEC_BATCH, DEC_SEQ, D_MODEL), 1.0),
        "cache_k": nrm(ks[2], (DEPTH, DEC_BATCH, WINDOW, KV_HEADS, HEAD_DIM), 1.0),
        "cache_v": nrm(ks[3], (DEPTH, DEC_BATCH, WINDOW, KV_HEADS, HEAD_DIM), 1.0),
        "norm_mix_w": gain(ks[4], (DEPTH, D_MODEL)),
        "w_in": nrm(ks[5], (DEPTH, D_MODEL, D_IN), D_MODEL ** -0.5),
        "q_norm_w": gain(ks[6], (DEPTH, HEAD_DIM)),
        "k_norm_w": gain(ks[7], (DEPTH, HEAD_DIM)),
        "sinks": nrm(ks[8], (DEPTH, ATTN_HEADS), 0.5),
        "gm_v_norm_w": gain(ks[9], (DEPTH, GM_HEADS, GM_HEAD_DIM)),
        "w_spatial": nrm(ks[10], (DEPTH, GM_HEADS, CHUNK, CHUNK), CHUNK ** -0.5),
        "b_spatial": (1.0 + 0.1 * jax.random.normal(ks[11], (DEPTH, GM_HEADS, CHUNK))).astype(f32),
        "out_norm_w": gain(ks[12], (DEPTH, D_MIX)),
        "w_out": nrm(ks[13], (DEPTH, D_MIX, D_MODEL), D_MIX ** -0.5),
        "norm_ffn_w": gain(ks[14], (DEPTH, D_MODEL)),
        "w_query": nrm(ks[15], (DEPTH, D_MODEL, PEER_HEADS * D_KEY), D_MODEL ** -0.5),
        "sub_keys": nrm(ks[16], (DEPTH, PEER_HEADS, 2, N_KEYS, HALF_KEY), HALF_KEY ** -0.5),
        "expert_u": nrm(ks[17], (DEPTH, N_EXPERTS, D_MODEL), D_MODEL ** -0.5),
        "expert_v": nrm(ks[18], (DEPTH, N_EXPERTS, D_MODEL), D_MODEL ** -0.5),
    }


def reference(x_prompt, x_sample, cache_k, cache_v, norm_mix_w, w_in, q_norm_w, k_norm_w, sinks,
              gm_v_norm_w, w_spatial, b_spatial, out_norm_w, w_out, norm_ffn_w, w_query, sub_keys,
              expert_u, expert_v):
    pos_p = jnp.arange(SEQ, dtype=jnp.int32)
    pos_s = PAST_LEN + jnp.arange(DEC_SEQ, dtype=jnp.int32)
    nb = SEQ // WINDOW
    qi = jnp.arange(WINDOW)[:, None]
    kj = jnp.arange(2 * WINDOW)[None, :]
    diff = qi + WINDOW - kj
    band = (diff >= 0) & (diff < WINDOW)
    mask_p = band[None] & ((jnp.arange(nb)[:, None, None] > 0) | (kj[None] >= WINDOW))
    si = jnp.arange(DEC_SEQ)[:, None]
    sj = jnp.arange(WINDOW + DEC_SEQ)[None, :]
    sdiff = si + WINDOW - sj
    mask_s = ((sdiff >= 0) & (sdiff < WINDOW))[None]

    h_p, h_s = x_prompt, x_sample
    nk_p, nv_p, nk_s, nv_s, ngv_s = [], [], [], [], []
    for layer in range(DEPTH):
        xn = rms_norm(h_p, norm_mix_w[layer])
        q, k, v, u, gv = mixer_inputs(xn, pos_p, w_in[layer], q_norm_w[layer], k_norm_w[layer], gm_v_norm_w[layer])
        kb = k.reshape(BATCH, nb, WINDOW, KV_HEADS, HEAD_DIM)
        vb = v.reshape(BATCH, nb, WINDOW, KV_HEADS, HEAD_DIM)
        pad_blk = ((0, 0), (1, 0), (0, 0), (0, 0), (0, 0))
        k_band = jnp.concatenate([jnp.pad(kb[:, :-1], pad_blk), kb], axis=2)
        v_band = jnp.concatenate([jnp.pad(vb[:, :-1], pad_blk), vb], axis=2)
        attn = sink_attention(q.reshape(BATCH, nb, WINDOW, ATTN_HEADS, HEAD_DIM), k_band, v_band,
                              sinks[layer], mask_p).reshape(BATCH, SEQ, ATTN_WIDTH)
        gm = spatial_gate(u, gv.reshape(BATCH, SEQ // CHUNK, CHUNK, GM_HEADS, GM_HEAD_DIM),
                          w_spatial[layer], b_spatial[layer])
        h_p = h_p + merge_heads(attn, gm, out_norm_w[layer], w_out[layer])
        h_p = h_p + peer(rms_norm(h_p, norm_ffn_w[layer]), w_query[layer], sub_keys[layer], expert_u[layer], expert_v[layer])
        nk_p.append(k[:, -WINDOW:])
        nv_p.append(v[:, -WINDOW:])
        xn = rms_norm(h_s, norm_mix_w[layer])
        q, k, v, u, gv = mixer_inputs(xn, pos_s, w_in[layer], q_norm_w[layer], k_norm_w[layer], gm_v_norm_w[layer])
        k_all = jnp.concatenate([cache_k[layer].astype(k.dtype), k], axis=1)
        v_all = jnp.concatenate([cache_v[layer].astype(v.dtype), v], axis=1)
        attn = sink_attention(q[:, None], k_all[:, None], v_all[:, None], sinks[layer],
                              mask_s).reshape(DEC_BATCH, DEC_SEQ, ATTN_WIDTH)
        gm = spatial_gate(u, gv[:, None], w_spatial[layer], b_spatial[layer])
        h_s = h_s + merge_heads(attn, gm, out_norm_w[layer], w_out[layer])
        h_s = h_s + peer(rms_norm(h_s, norm_ffn_w[layer]), w_query[layer], sub_keys[layer], expert_u[layer], expert_v[layer])
        nk_s.append(k_all[:, -WINDOW:])
        nv_s.append(v_all[:, -WINDOW:])
        ngv_s.append(gv)
    return (h_p, h_s, jnp.stack(nk_p), jnp.stack(nv_p), jnp.stack(nk_s), jnp.stack(nv_s), jnp.stack(ngv_s))
```

```python
import functools

import jax
import jax.numpy as jnp
import numpy as np
from jax import lax
from jax.experimental import pallas as pl
from jax.experimental.pallas import tpu as pltpu

F32 = jnp.float32
BF16 = jnp.bfloat16

D_MODEL = 1024
HEAD_DIM = 64
ATTN_HEADS = 8
KV_HEADS = 2
GQA_GROUP = ATTN_HEADS // KV_HEADS
WINDOW = 128
ROPE_THETA = 10000.0
GM_HEADS = 8
GM_HEAD_DIM = 64
CHUNK = 128
ATTN_WIDTH = ATTN_HEADS * HEAD_DIM
KV_WIDTH = KV_HEADS * HEAD_DIM
GM_WIDTH = GM_HEADS * GM_HEAD_DIM
D_MIX = ATTN_WIDTH + GM_WIDTH
D_IN = ATTN_WIDTH + 2 * KV_WIDTH + 2 * GM_WIDTH
PEER_HEADS = 8
N_KEYS = 128
N_EXPERTS = N_KEYS * N_KEYS
D_KEY = 256
HALF_KEY = D_KEY // 2
TOPK = 16
EPS = 1e-6
NEG_INF = -1e30

LANES = 128
SUBLANES = 8
VMEM_LIMIT_BYTES = 56 * 1024 * 1024


def _gelu(x):
    return 0.5 * x * (1.0 + lax.erf(x * np.float32(np.sqrt(0.5))))


def _rms_rows(x, w):
    ms = jnp.mean(x * x, axis=-1, keepdims=True)
    return x * lax.rsqrt(ms + EPS) * w


def _bitonic_merge_desc(xs):
    xs = list(xs)
    n = len(xs)
    j = n // 2
    while j >= 1:
        for i in range(n):
            l = i ^ j
            if l > i:
                a, b = xs[i], xs[l]
                xs[i] = jnp.maximum(a, b)
                xs[l] = jnp.minimum(a, b)
        j //= 2
    return xs


def _bitonic_sort_desc(xs):
    xs = list(xs)
    n = len(xs)
    k = 2
    while k <= n:
        j = k // 2
        while j >= 1:
            for i in range(n):
                l = i ^ j
                if l > i:
                    a, b = xs[i], xs[l]
                    hi, lo = jnp.maximum(a, b), jnp.minimum(a, b)
                    if (i & k) == 0:
                        xs[i], xs[l] = hi, lo
                    else:
                        xs[i], xs[l] = lo, hi
            j //= 2
        k *= 2
    return xs


def _top16_rows_desc(s):
    xs = _bitonic_sort_desc([s[SUBLANES * r:SUBLANES * (r + 1), :] for r in range(TOPK)])
    for shift in (4, 2, 1):
        other = [pltpu.roll(x, shift, axis=0) for x in xs]
        xs = _bitonic_merge_desc([jnp.maximum(xs[i], other[TOPK - 1 - i]) for i in range(TOPK)])
    return xs


_CAND_PAIRS = [(i, j) for i in range(TOPK) for j in range(TOPK) if (i + 1) * (j + 1) <= TOPK]


def _peer_kernel(h_ref, nw_ref, wqt_ref, sk_ref, u_ref, vt_ref, y_ref,
                 hnT_ref, qT_ref, s1_ref, s2_ref, e1_ref, e2_ref, thr_ref, rows_ref, a_ref, g_ref,
                 acc_ref,
                 *, tn, et, strip):
    j = pl.program_id(1)
    n_sub = et // N_KEYS
    n_lt = tn // LANES

    @pl.when(j == 0)
    def _prologue():
        hn = _rms_rows(h_ref[...], nw_ref[...])
        hnT = hn.T.astype(BF16)
        hnT_ref[...] = hnT
        qT_ref[...] = jnp.dot(wqt_ref[...], hnT, preferred_element_type=F32)
        sub = lax.broadcasted_iota(jnp.int32, (SUBLANES, tn), 0)

        def head_body(hd, tops):
            tops = list(tops)
            for p, s_ref in enumerate((s1_ref, s2_ref)):
                c0 = pl.multiple_of((hd * 2 + p) * HALF_KEY, HALF_KEY)
                q = qT_ref[pl.ds(c0, HALF_KEY), :].astype(BF16)
                s = jnp.dot(sk_ref[hd * 2 + p], q, preferred_element_type=F32)
                s_ref[hd] = s
                ranked = _top16_rows_desc(s)
                for r in range(TOPK):
                    tops[p * TOPK + r] = jnp.where(sub == hd, ranked[r], tops[p * TOPK + r])
            return tuple(tops)

        tops = lax.fori_loop(0, PEER_HEADS, head_body,
                             tuple(jnp.zeros((SUBLANES, tn), F32) for _ in range(2 * TOPK)))
        top1, top2 = tops[:TOPK], tops[TOPK:]
        cands = [top1[i] + top2[jj] for (i, jj) in _CAND_PAIRS]
        cands += [jnp.full((SUBLANES, tn), -jnp.inf, F32)] * (64 - len(cands))
        best = _bitonic_sort_desc(cands)[:TOPK]
        m = best[0]
        z = jnp.exp(best[0] - m)
        for r in range(1, TOPK):
            z = z + jnp.exp(best[r] - m)
        thr = best[TOPK - 1]
        inv_z = 1.0 / z
        for hd in range(PEER_HEADS):
            thr_ref[hd] = jnp.broadcast_to(thr[hd:hd + 1, :], (SUBLANES, tn))
            e1_ref[hd] = jnp.exp(s1_ref[hd] - top1[0][hd:hd + 1, :])
            e2_ref[hd] = jnp.exp(s2_ref[hd] - top2[0][hd:hd + 1, :]) * inv_z[hd:hd + 1, :]
        acc_ref[...] = jnp.zeros_like(acc_ref)

    a_ref[...] = jnp.dot(u_ref[...], hnT_ref[...], preferred_element_type=F32)

    i1_base = pl.multiple_of(j * n_sub, SUBLANES)
    for r in range(n_sub):
        for hd in range(PEER_HEADS):
            g8 = pl.ds(i1_base + (r // SUBLANES) * SUBLANES, SUBLANES)
            o = r % SUBLANES
            rows_ref[hd] = jnp.broadcast_to(s1_ref[hd, g8, :][o:o + 1, :], (SUBLANES, tn))
            rows_ref[PEER_HEADS + hd] = jnp.broadcast_to(e1_ref[hd, g8, :][o:o + 1, :], (SUBLANES, tn))

        def strip_body(sidx, carry, r=r):
            row0 = pl.multiple_of(sidx * strip, strip)
            erow = pl.multiple_of(r * N_KEYS + sidx * strip, strip)
            for t in range(n_lt):
                lanes = slice(t * LANES, (t + 1) * LANES)
                ws = [jnp.zeros((SUBLANES, LANES), F32)] * (strip // SUBLANES)
                for hd in range(PEER_HEADS):
                    s1row = rows_ref[hd, :, lanes]
                    e1row = rows_ref[PEER_HEADS + hd, :, lanes]
                    th = thr_ref[hd, :, lanes]
                    for v in range(strip // SUBLANES):
                        rr = pl.ds(row0 + v * SUBLANES, SUBLANES)
                        s2 = s2_ref[hd, rr, lanes]
                        e2 = e2_ref[hd, rr, lanes]
                        ws[v] = ws[v] + jnp.where(s2 + s1row >= th, e2 * e1row, 0.0)
                w = jnp.concatenate(ws, axis=0)
                a = a_ref[pl.ds(erow, strip), lanes]
                g_ref[pl.ds(erow, strip), lanes] = (w * _gelu(a)).astype(BF16)
            return carry

        lax.fori_loop(0, N_KEYS // strip, strip_body, 0)

    acc_ref[...] += jnp.dot(vt_ref[...], g_ref[...], preferred_element_type=F32)

    @pl.when(j == pl.num_programs(1) - 1)
    def _epilogue():
        y_ref[...] = h_ref[...] + acc_ref[...].T


def _peer_call(h, norm_w, wqt, skeys, u_bf, vt_bf, *, tn, et, strip=32):
    n, d = h.shape
    n_exp = u_bf.shape[0]
    assert n % tn == 0 and n_exp % et == 0 and et % (SUBLANES * N_KEYS) == 0 and tn % LANES == 0
    grid = (n // tn, n_exp // et)
    kern = functools.partial(_peer_kernel, tn=tn, et=et, strip=strip)
    table = pltpu.VMEM((PEER_HEADS, N_KEYS, tn), F32)
    return pl.pallas_call(
        kern,
        out_shape=jax.ShapeDtypeStruct((n, d), F32),
        grid=grid,
        in_specs=[
            pl.BlockSpec((tn, d), lambda i, j: (i, 0)),
            pl.BlockSpec((1, d), lambda i, j: (0, 0)),
            pl.BlockSpec(wqt.shape, lambda i, j: (0, 0)),
            pl.BlockSpec(skeys.shape, lambda i, j: (0, 0, 0)),
            pl.BlockSpec((et, d), lambda i, j: (j, 0)),
            pl.BlockSpec((d, et), lambda i, j: (0, j)),
        ],
        out_specs=pl.BlockSpec((tn, d), lambda i, j: (i, 0)),
        scratch_shapes=[
            pltpu.VMEM((d, tn), BF16),
            pltpu.VMEM((wqt.shape[0], tn), F32),
            table, table, table, table,
            pltpu.VMEM((PEER_HEADS, SUBLANES, tn), F32),
            pltpu.VMEM((2 * PEER_HEADS, SUBLANES, tn), F32),
            pltpu.VMEM((et, tn), F32),
            pltpu.VMEM((et, tn), BF16),
            pltpu.VMEM((d, tn), F32),
        ],
        compiler_params=pltpu.CompilerParams(
            dimension_semantics=("arbitrary", "arbitrary"),
            vmem_limit_bytes=VMEM_LIMIT_BYTES),
        name="peer",
    )(h, norm_w, wqt, skeys, u_bf, vt_bf)


def _group_mean_sq(x, bd):
    x2 = x * x
    hi = x2.astype(BF16)
    lo = (x2 - hi.astype(F32)).astype(BF16)
    return (jnp.dot(hi, bd, preferred_element_type=F32) + jnp.dot(lo, bd, preferred_element_type=F32))


def _head_norm(x, w, bd):
    return x * lax.rsqrt(_group_mean_sq(x, bd) + EPS) * w


def _rope(x, cos, sin_signed):
    width = x.shape[-1]
    half = HEAD_DIM // 2
    lane = lax.broadcasted_iota(jnp.int32, x.shape, 1)
    first_half = (lane % HEAD_DIM) < half
    rot = jnp.where(first_half, pltpu.roll(x, width - half, axis=1), pltpu.roll(x, half, axis=1))
    return x * cos + rot * sin_signed


def _project_in(x, nmw, win, qnw, knw, gvw, cos, sin_signed, bd):
    xn = _rms_rows(x, nmw).astype(BF16)
    z = jnp.dot(xn, win, preferred_element_type=F32)
    o1, o2, o3, o4 = ATTN_WIDTH, ATTN_WIDTH + KV_WIDTH, ATTN_WIDTH + 2 * KV_WIDTH, D_IN - GM_WIDTH
    reps = ATTN_WIDTH // LANES
    cos_q = jnp.concatenate([cos] * reps, axis=1)
    sin_q = jnp.concatenate([sin_signed] * reps, axis=1)
    q = _rope(_head_norm(z[:, :o1], qnw, bd), cos_q, sin_q) * (HEAD_DIM ** -0.5)
    k = _rope(_head_norm(z[:, o1:o2], knw, bd[:KV_WIDTH, :KV_WIDTH]), cos, sin_signed)
    v = z[:, o2:o3]
    ug = _gelu(z[:, o3:o4])
    gvn = _head_norm(_gelu(z[:, o4:]), gvw, bd)
    return q, k, v, ug, gvn


def _sink_softmax_pv(s, sink, v_bf):
    m = jnp.maximum(jnp.max(s, axis=-1, keepdims=True), sink)
    p = jnp.exp(s - m)
    denom = jnp.sum(p, axis=-1, keepdims=True) + jnp.exp(sink - m)
    return jnp.dot((p / denom).astype(BF16), v_bf, preferred_element_type=F32)


def _merge(x, attn, gm, onw, wout):
    a = _rms_rows(attn, onw[:, :ATTN_WIDTH])
    g = _rms_rows(gm, onw[:, ATTN_WIDTH:])
    cat = jnp.concatenate([a, g], axis=1).astype(BF16)
    return x + jnp.dot(cat, wout, preferred_element_type=F32)


def _mixer_prompt_kernel(sinks_ref, x_ref, nmw_ref, win_ref, qnw_ref, knw_ref, gvw_ref, cos_ref,
                         sin_ref, bd_ref, wsp_ref, bsp_ref, onw_ref, wout_ref,
                         h_ref, kout_ref, vout_ref,
                         q_ref, kext_ref, vext_ref, ug_ref, gvn_ref, attn_ref, gm_ref, wtril_ref,
                         *, tb):
    i = pl.program_id(0)
    nsb = tb // WINDOW

    @pl.when(i == 0)
    def _init():
        kext_ref[0:WINDOW, :] = jnp.zeros((WINDOW, KV_WIDTH), BF16)
        vext_ref[0:WINDOW, :] = jnp.zeros((WINDOW, KV_WIDTH), BF16)
        row = lax.broadcasted_iota(jnp.int32, (CHUNK, CHUNK), 0)
        col = lax.broadcasted_iota(jnp.int32, (CHUNK, CHUNK), 1)
        for hd in range(GM_HEADS):
            wtril_ref[hd] = jnp.where(row >= col, wsp_ref[hd], 0.0).astype(BF16)

    x = x_ref[...]
    q, k, v, ug, gvn = _project_in(x, nmw_ref[...], win_ref[...], qnw_ref[...], knw_ref[...],
                                   gvw_ref[...], cos_ref[...], sin_ref[...], bd_ref[...])
    q_ref[...] = q.astype(BF16)
    kext_ref[WINDOW:, :] = k.astype(BF16)
    vext_ref[WINDOW:, :] = v.astype(BF16)
    ug_ref[...] = ug
    gvn_ref[...] = gvn.astype(BF16)
    kout_ref[...] = k[tb - WINDOW:, :]
    vout_ref[...] = v[tb - WINDOW:, :]

    qi = lax.broadcasted_iota(jnp.int32, (WINDOW, 2 * WINDOW), 0)
    kj = lax.broadcasted_iota(jnp.int32, (WINDOW, 2 * WINDOW), 1)
    band = (kj > qi) & (kj <= qi + WINDOW)
    lane = lax.broadcasted_iota(jnp.int32, (CHUNK, LANES), 1)

    def sub_block(b, carry):
        r0 = pl.multiple_of(b * WINDOW, WINDOW)
        rows = pl.ds(r0, WINDOW)
        kb = kext_ref[pl.ds(r0, 2 * WINDOW), :]
        vb = vext_ref[pl.ds(r0, 2 * WINDOW), :]
        first = jnp.logical_and(i == 0, b == 0)
        mask = band & (kj >= jnp.where(first, WINDOW, 0))
        qb = q_ref[rows, :]
        outs = []
        for hd in range(ATTN_HEADS):
            g = hd // GQA_GROUP
            qh = qb[:, hd * HEAD_DIM:(hd + 1) * HEAD_DIM]
            kg = kb[:, g * HEAD_DIM:(g + 1) * HEAD_DIM]
            vg = vb[:, g * HEAD_DIM:(g + 1) * HEAD_DIM]
            s = lax.dot_general(qh, kg, (((1,), (1,)), ((), ())), preferred_element_type=F32)
            s = jnp.where(mask, s, NEG_INF)
            outs.append(_sink_softmax_pv(s, sinks_ref[hd], vg))
        attn_ref[rows, :] = jnp.concatenate(outs, axis=1)
        gvb = gvn_ref[rows, :]
        tiles = []
        for t in range(GM_WIDTH // LANES):
            gt = gvb[:, t * LANES:(t + 1) * LANES]
            y0 = jnp.dot(wtril_ref[2 * t], gt, preferred_element_type=F32)
            y1 = jnp.dot(wtril_ref[2 * t + 1], gt, preferred_element_type=F32)
            tiles.append(jnp.where(lane < GM_HEAD_DIM, y0, y1))
        mixed = jnp.concatenate(tiles, axis=1) + bsp_ref[...]
        gm_ref[rows, :] = ug_ref[rows, :] * mixed
        return carry

    lax.fori_loop(0, nsb, sub_block, 0)

    h_ref[...] = _merge(x, attn_ref[...], gm_ref[...], onw_ref[...], wout_ref[...])
    kext_ref[0:WINDOW, :] = kext_ref[tb:tb + WINDOW, :]
    vext_ref[0:WINDOW, :] = vext_ref[tb:tb + WINDOW, :]


def _const_spec(shape):
    nd = len(shape)
    return pl.BlockSpec(shape, lambda i: (0,) * nd)


def _mixer_prompt_call(x, sinks, nmw, win, qnw, knw, gvw, cos, sin, bd, wsp, bsp, onw, wout, *, tb):
    n, d = x.shape
    assert n % tb == 0 and tb % WINDOW == 0
    kern = functools.partial(_mixer_prompt_kernel, tb=tb)
    return pl.pallas_call(
        kern,
        out_shape=(jax.ShapeDtypeStruct((n, d), F32),
                   jax.ShapeDtypeStruct((WINDOW, KV_WIDTH), F32),
                   jax.ShapeDtypeStruct((WINDOW, KV_WIDTH), F32)),
        grid=(n // tb,),
        in_specs=[
            pl.BlockSpec(memory_space=pltpu.SMEM),
            pl.BlockSpec((tb, d), lambda i: (i, 0)),
            _const_spec(nmw.shape), _const_spec(win.shape), _const_spec(qnw.shape),
            _const_spec(knw.shape), _const_spec(gvw.shape),
            pl.BlockSpec((tb, LANES), lambda i: (i, 0)),
            pl.BlockSpec((tb, LANES), lambda i: (i, 0)),
            _const_spec(bd.shape), _const_spec(wsp.shape), _const_spec(bsp.shape),
            _const_spec(onw.shape), _const_spec(wout.shape),
        ],
        out_specs=(pl.BlockSpec((tb, d), lambda i: (i, 0)),
                   _const_spec((WINDOW, KV_WIDTH)), _const_spec((WINDOW, KV_WIDTH))),
        scratch_shapes=[
            pltpu.VMEM((tb, ATTN_WIDTH), BF16),
            pltpu.VMEM((tb + WINDOW, KV_WIDTH), BF16),
            pltpu.VMEM((tb + WINDOW, KV_WIDTH), BF16),
            pltpu.VMEM((tb, GM_WIDTH), F32),
            pltpu.VMEM((tb, GM_WIDTH), BF16),
            pltpu.VMEM((tb, ATTN_WIDTH), F32),
            pltpu.VMEM((tb, GM_WIDTH), F32),
            pltpu.VMEM((GM_HEADS, CHUNK, CHUNK), BF16),
        ],
        compiler_params=pltpu.CompilerParams(
            dimension_semantics=("arbitrary",), vmem_limit_bytes=VMEM_LIMIT_BYTES),
        name="mixer_prompt",
    )(sinks, x, nmw, win, qnw, knw, gvw, cos, sin, bd, wsp, bsp, onw, wout)


SAMPLE_CHUNK = 8


def _mixer_sample_kernel(sinks_ref, x_ref, ck_ref, cv_ref, nmw_ref, win_ref, qnw_ref, knw_ref,
                         gvw_ref, cos_ref, sin_ref, bd_ref, w00_ref, b0_ref, onw_ref, wout_ref,
                         h_ref, kout_ref, vout_ref, gvout_ref, attn_ref):
    nb = x_ref.shape[0]
    x = x_ref[...]
    q, k, v, ug, gvn = _project_in(x, nmw_ref[...], win_ref[...], qnw_ref[...], knw_ref[...],
                                   gvw_ref[...], cos_ref[...], sin_ref[...], bd_ref[...])
    kout_ref[...] = k
    vout_ref[...] = v
    gvout_ref[...] = gvn

    cb = SAMPLE_CHUNK
    lane = lax.broadcasted_iota(jnp.int32, (cb, LANES), 1)
    jrow = lax.broadcasted_iota(jnp.int32, (WINDOW, LANES), 0)
    rseq = lax.broadcasted_iota(jnp.int32, (GQA_GROUP * cb, cb * WINDOW), 0) % cb
    cseq = lax.broadcasted_iota(jnp.int32, (GQA_GROUP * cb, cb * WINDOW), 1) // WINDOW
    same_seq = rseq == cseq
    for c in range(nb // cb):
        rows = slice(c * cb, (c + 1) * cb)
        k_eff = jnp.concatenate(
            [jnp.where(jrow == 0, k[b:b + 1, :], ck_ref[b]) for b in range(c * cb, (c + 1) * cb)],
            axis=0).astype(BF16)
        v_eff = jnp.concatenate(
            [jnp.where(jrow == 0, v[b:b + 1, :], cv_ref[b]) for b in range(c * cb, (c + 1) * cb)],
            axis=0).astype(BF16)
        outs = [None] * ATTN_HEADS
        for g in range(KV_HEADS):
            qz = []
            for hh in range(GQA_GROUP):
                hd = g * GQA_GROUP + hh
                tile = q[rows, (hd // 2) * LANES:(hd // 2 + 1) * LANES]
                if hd % 2 != g:
                    tile = pltpu.roll(tile, HEAD_DIM, axis=1)
                qz.append(jnp.where((lane // HEAD_DIM) == g, tile, 0.0))
            qz = jnp.concatenate(qz, axis=0).astype(BF16)
            s = lax.dot_general(qz, k_eff, (((1,), (1,)), ((), ())), preferred_element_type=F32)
            s = jnp.where(same_seq, s, NEG_INF)
            for hh in range(GQA_GROUP):
                hd = g * GQA_GROUP + hh
                o = _sink_softmax_pv(s[hh * cb:(hh + 1) * cb, :], sinks_ref[hd], v_eff)
                outs[hd] = o[:, g * HEAD_DIM:(g + 1) * HEAD_DIM]
        attn_ref[rows, :] = jnp.concatenate(outs, axis=1)

    gm = ug * (w00_ref[...] * gvn + b0_ref[...])
    h_ref[...] = _merge(x, attn_ref[...], gm, onw_ref[...], wout_ref[...])


def _mixer_sample_call(x, sinks, ck, cv, nmw, win, qnw, knw, gvw, cos, sin, bd, w00, b0, onw, wout):
    nb, d = x.shape
    assert nb % SAMPLE_CHUNK == 0
    vmem = pl.BlockSpec(memory_space=pltpu.VMEM)
    return pl.pallas_call(
        _mixer_sample_kernel,
        out_shape=(jax.ShapeDtypeStruct((nb, d), F32),
                   jax.ShapeDtypeStruct((nb, KV_WIDTH), F32),
                   jax.ShapeDtypeStruct((nb, KV_WIDTH), F32),
                   jax.ShapeDtypeStruct((nb, GM_WIDTH), F32)),
        in_specs=[pl.BlockSpec(memory_space=pltpu.SMEM)] + [vmem] * 15,
        out_specs=(vmem, vmem, vmem, vmem),
        scratch_shapes=[pltpu.VMEM((nb, ATTN_WIDTH), F32)],
        compiler_params=pltpu.CompilerParams(vmem_limit_bytes=VMEM_LIMIT_BYTES),
        name="mixer_sample",
    )(sinks, x, ck, cv, nmw, win, qnw, knw, gvw, cos, sin, bd, w00, b0, onw, wout)


def _rope_tables(pos):
    half = HEAD_DIM // 2
    inv = ROPE_THETA ** (-jnp.arange(half, dtype=F32) / half)
    ang = pos.astype(F32)[:, None] * inv[None, :]
    cos, sin = jnp.cos(ang), jnp.sin(ang)
    return (jnp.concatenate([cos, cos, cos, cos], axis=1),
            jnp.concatenate([-sin, sin, -sin, sin], axis=1))


PROMPT_BLOCK = 512
PEER_TOKENS = 512
PEER_EXPERT_TILE = 1024


def kernel(x_prompt, x_sample, cache_k, cache_v, norm_mix_w, w_in, q_norm_w, k_norm_w, sinks,
           gm_v_norm_w, w_spatial, b_spatial, out_norm_w, w_out, norm_ffn_w, w_query, sub_keys,
           expert_u, expert_v):
    depth, batch, seq = w_in.shape[0], x_prompt.shape[0], x_prompt.shape[1]
    dec_batch, dec_seq = x_sample.shape[0], x_sample.shape[1]
    assert depth == 1 and batch == 1 and dec_seq == 1
    past_len = seq
    layer = 0

    nmw = norm_mix_w[layer][None, :]
    win = w_in[layer].astype(BF16)
    qnw = jnp.tile(q_norm_w[layer], ATTN_HEADS)[None, :]
    knw = jnp.tile(k_norm_w[layer], KV_HEADS)[None, :]
    gvw = gm_v_norm_w[layer].reshape(1, GM_WIDTH)
    group = jnp.arange(ATTN_WIDTH) // HEAD_DIM
    bd = jnp.where(group[:, None] == group[None, :], 1.0 / HEAD_DIM, 0.0).astype(BF16)
    wsp = w_spatial[layer]
    bsp = jnp.repeat(b_spatial[layer].T, GM_HEAD_DIM, axis=1)
    w00 = jnp.repeat(w_spatial[layer][:, 0, 0], GM_HEAD_DIM)[None, :]
    b0 = jnp.repeat(b_spatial[layer][:, 0], GM_HEAD_DIM)[None, :]
    onw = out_norm_w[layer][None, :]
    wout = w_out[layer].astype(BF16)
    nfw = norm_ffn_w[layer][None, :]
    wqt = w_query[layer].T.astype(BF16)
    skeys = sub_keys[layer].reshape(PEER_HEADS * 2, N_KEYS, HALF_KEY).astype(BF16)
    u_bf = expert_u[layer].astype(BF16)
    vt_bf = expert_v[layer].T.astype(BF16)
    sinks_l = sinks[layer]
    cos_p, sin_p = _rope_tables(jnp.arange(seq, dtype=jnp.int32))
    cos_s, sin_s = _rope_tables(past_len + jnp.zeros((dec_batch,), jnp.int32))

    h_p, nk_p, nv_p = _mixer_prompt_call(x_prompt[0], sinks_l, nmw, win, qnw, knw, gvw, cos_p, sin_p,
                                         bd, wsp, bsp, onw, wout, tb=PROMPT_BLOCK)
    y_p = _peer_call(h_p, nfw, wqt, skeys, u_bf, vt_bf, tn=PEER_TOKENS, et=PEER_EXPERT_TILE)

    ck = cache_k[layer].reshape(dec_batch, WINDOW, KV_WIDTH)
    cv = cache_v[layer].reshape(dec_batch, WINDOW, KV_WIDTH)
    h_s, k_s, v_s, gv_s = _mixer_sample_call(x_sample[:, 0], sinks_l, ck, cv, nmw, win, qnw, knw, gvw,
                                             cos_s, sin_s, bd, w00, b0, onw, wout)
    y_s = _peer_call(h_s, nfw, wqt, skeys, u_bf, vt_bf, tn=dec_batch, et=PEER_EXPERT_TILE)

    nk_s = jnp.concatenate([ck[:, 1:], k_s[:, None, :]], axis=1)
    nv_s = jnp.concatenate([cv[:, 1:], v_s[:, None, :]], axis=1)
    return (y_p[None],
            y_s[:, None, :],
            nk_p.reshape(1, 1, WINDOW, KV_HEADS, HEAD_DIM),
            nv_p.reshape(1, 1, WINDOW, KV_HEADS, HEAD_DIM),
            nk_s.reshape(1, dec_batch, WINDOW, KV_HEADS, HEAD_DIM),
            nv_s.reshape(1, dec_batch, WINDOW, KV_HEADS, HEAD_DIM),
            gv_s.reshape(1, dec_batch, 1, GM_HEADS, GM_HEAD_DIM))
```

```python
import functools

import jax
import jax.numpy as jnp
import numpy as np
from jax import lax
from jax.experimental import pallas as pl
from jax.experimental.pallas import tpu as pltpu

F32 = jnp.float32
BF16 = jnp.bfloat16

D_MODEL = 1024
HEAD_DIM = 64
ATTN_HEADS = 8
KV_HEADS = 2
GQA_GROUP = ATTN_HEADS // KV_HEADS
WINDOW = 128
ROPE_THETA = 10000.0
GM_HEADS = 8
GM_HEAD_DIM = 64
CHUNK = 128
ATTN_WIDTH = ATTN_HEADS * HEAD_DIM
KV_WIDTH = KV_HEADS * HEAD_DIM
GM_WIDTH = GM_HEADS * GM_HEAD_DIM
D_MIX = ATTN_WIDTH + GM_WIDTH
D_IN = ATTN_WIDTH + 2 * KV_WIDTH + 2 * GM_WIDTH
PEER_HEADS = 8
N_KEYS = 128
N_EXPERTS = N_KEYS * N_KEYS
D_KEY = 256
HALF_KEY = D_KEY // 2
TOPK = 16
EPS = 1e-6
NEG_INF = -1e30

LANES = 128
SUBLANES = 8
VMEM_LIMIT_BYTES = 56 * 1024 * 1024


def _gelu(x):
    return 0.5 * x * (1.0 + lax.erf(x * np.float32(np.sqrt(0.5))))


def _rms_rows(x, w):
    ms = jnp.mean(x * x, axis=-1, keepdims=True)
    return x * lax.rsqrt(ms + EPS) * w


def _bitonic_merge_desc(xs):
    xs = list(xs)
    n = len(xs)
    j = n // 2
    while j >= 1:
        for i in range(n):
            l = i ^ j
            if l > i:
                a, b = xs[i], xs[l]
                xs[i] = jnp.maximum(a, b)
                xs[l] = jnp.minimum(a, b)
        j //= 2
    return xs


def _bitonic_sort_desc(xs):
    xs = list(xs)
    n = len(xs)
    k = 2
    while k <= n:
        j = k // 2
        while j >= 1:
            for i in range(n):
                l = i ^ j
                if l > i:
                    a, b = xs[i], xs[l]
                    hi, lo = jnp.maximum(a, b), jnp.minimum(a, b)
                    if (i & k) == 0:
                        xs[i], xs[l] = hi, lo
                    else:
                        xs[i], xs[l] = lo, hi
            j //= 2
        k *= 2
    return xs


def _top16_rows_desc(s):
    xs = _bitonic_sort_desc([s[SUBLANES * r:SUBLANES * (r + 1), :] for r in range(TOPK)])
    for shift in (4, 2, 1):
        other = [pltpu.roll(x, shift, axis=0) for x in xs]
        xs = _bitonic_merge_desc([jnp.maximum(xs[i], other[TOPK - 1 - i]) for i in range(TOPK)])
    return xs


_CAND_PAIRS = [(i, j) for i in range(TOPK) for j in range(TOPK) if (i + 1) * (j + 1) <= TOPK]


def _peer_kernel(h_ref, nw_ref, wqt_ref, sk_ref, u_ref, vt_ref, y_ref,
                 hnT_ref, qT_ref, s1_ref, s2_ref, e1_ref, se_ref, thr_ref, rows_ref, a_ref, g_ref,
                 acc_ref,
                 *, tn, et, strip):
    j = pl.program_id(1)
    n_sub = et // N_KEYS
    n_lt = tn // LANES

    @pl.when(j == 0)
    def _prologue():
        hn = _rms_rows(h_ref[...], nw_ref[...])
        hnT = hn.T.astype(BF16)
        hnT_ref[...] = hnT
        qT_ref[...] = jnp.dot(wqt_ref[...], hnT, preferred_element_type=F32)
        sub = lax.broadcasted_iota(jnp.int32, (SUBLANES, tn), 0)

        def head_body(hd, tops):
            tops = list(tops)
            for p, s_ref in enumerate((s1_ref, s2_ref)):
                c0 = pl.multiple_of((hd * 2 + p) * HALF_KEY, HALF_KEY)
                q = qT_ref[pl.ds(c0, HALF_KEY), :].astype(BF16)
                s = jnp.dot(sk_ref[hd * 2 + p], q, preferred_element_type=F32)
                s_ref[hd] = s
                ranked = _top16_rows_desc(s)
                for r in range(TOPK):
                    tops[p * TOPK + r] = jnp.where(sub == hd, ranked[r], tops[p * TOPK + r])
            return tuple(tops)

        tops = lax.fori_loop(0, PEER_HEADS, head_body,
                             tuple(jnp.zeros((SUBLANES, tn), F32) for _ in range(2 * TOPK)))
        top1, top2 = tops[:TOPK], tops[TOPK:]
        cands = [top1[i] + top2[jj] for (i, jj) in _CAND_PAIRS]
        cands += [jnp.full((SUBLANES, tn), -jnp.inf, F32)] * (64 - len(cands))
        best = _bitonic_sort_desc(cands)[:TOPK]
        m = best[0]
        z = jnp.exp(best[0] - m)
        for r in range(1, TOPK):
            z = z + jnp.exp(best[r] - m)
        thr = best[TOPK - 1]
        inv_z = 1.0 / z
        n_rg = N_KEYS // SUBLANES
        for hd in range(PEER_HEADS):
            e1_ref[hd] = jnp.exp(s1_ref[hd] - top1[0][hd:hd + 1, :])
            s2 = s2_ref[hd]
            e2 = jnp.exp(s2 - top2[0][hd:hd + 1, :]) * inv_z[hd:hd + 1, :]
            for t in range(n_lt):
                lanes = slice(t * LANES, (t + 1) * LANES)
                se_ref[hd, t, :, 0] = s2[:, lanes].reshape(n_rg, SUBLANES, LANES)
                se_ref[hd, t, :, 1] = e2[:, lanes].reshape(n_rg, SUBLANES, LANES)
                thr_ref[t, hd] = jnp.broadcast_to(thr[hd:hd + 1, lanes], (SUBLANES, LANES))
        acc_ref[...] = jnp.zeros_like(acc_ref)

    a_ref[...] = jnp.dot(u_ref[...], hnT_ref[...], preferred_element_type=F32)

    i1_base = pl.multiple_of(j * n_sub, SUBLANES)
    for r in range(n_sub):
        for hd in range(PEER_HEADS):
            g8 = pl.ds(i1_base + (r // SUBLANES) * SUBLANES, SUBLANES)
            o = r % SUBLANES
            s1g = s1_ref[hd, g8, :]
            e1g = e1_ref[hd, g8, :]
            for t in range(n_lt):
                lanes = slice(t * LANES, (t + 1) * LANES)
                rows_ref[t, hd, 0] = jnp.broadcast_to(s1g[o:o + 1, lanes], (SUBLANES, LANES))
                rows_ref[t, hd, 1] = jnp.broadcast_to(e1g[o:o + 1, lanes], (SUBLANES, LANES))

        def strip_body(sidx, carry, r=r):
            row0 = sidx * strip
            erow = r * N_KEYS + sidx * strip
            for t in range(n_lt):
                lanes = slice(t * LANES, (t + 1) * LANES)
                ws = [jnp.zeros((SUBLANES, LANES), F32)] * (strip // SUBLANES)
                for hd in range(PEER_HEADS):
                    s1row = rows_ref[t, hd, 0]
                    e1row = rows_ref[t, hd, 1]
                    th = thr_ref[t, hd]
                    for v in range(strip // SUBLANES):
                        rg = row0 // SUBLANES + v
                        s2 = se_ref[hd, t, rg, 0]
                        e2 = se_ref[hd, t, rg, 1]
                        ws[v] = ws[v] + jnp.where(s2 + s1row >= th, e2 * e1row, 0.0)
                w = jnp.concatenate(ws, axis=0)
                a = a_ref[pl.ds(erow, strip), lanes]
                g_ref[pl.ds(erow, strip), lanes] = (w * _gelu(a)).astype(BF16)
            return carry

        for sidx in range(N_KEYS // strip):
            strip_body(sidx, 0)

    acc_ref[...] += jnp.dot(vt_ref[...], g_ref[...], preferred_element_type=F32)

    @pl.when(j == pl.num_programs(1) - 1)
    def _epilogue():
        y_ref[...] = h_ref[...] + acc_ref[...].T


def _peer_call(h, norm_w, wqt, skeys, u_bf, vt_bf, *, tn, et, strip=32):
    n, d = h.shape
    n_exp = u_bf.shape[0]
    assert n % tn == 0 and n_exp % et == 0 and et % (SUBLANES * N_KEYS) == 0 and tn % LANES == 0
    grid = (n // tn, n_exp // et)
    kern = functools.partial(_peer_kernel, tn=tn, et=et, strip=strip)
    table = pltpu.VMEM((PEER_HEADS, N_KEYS, tn), F32)
    return pl.pallas_call(
        kern,
        out_shape=jax.ShapeDtypeStruct((n, d), F32),
        grid=grid,
        in_specs=[
            pl.BlockSpec((tn, d), lambda i, j: (i, 0)),
            pl.BlockSpec((1, d), lambda i, j: (0, 0)),
            pl.BlockSpec(wqt.shape, lambda i, j: (0, 0)),
            pl.BlockSpec(skeys.shape, lambda i, j: (0, 0, 0)),
            pl.BlockSpec((et, d), lambda i, j: (j, 0)),
            pl.BlockSpec((d, et), lambda i, j: (0, j)),
        ],
        out_specs=pl.BlockSpec((tn, d), lambda i, j: (i, 0)),
        scratch_shapes=[
            pltpu.VMEM((d, tn), BF16),
            pltpu.VMEM((wqt.shape[0], tn), F32),
            table, table, table,
            pltpu.VMEM((PEER_HEADS, tn // LANES, N_KEYS // SUBLANES, 2, SUBLANES, LANES), F32),
            pltpu.VMEM((tn // LANES, PEER_HEADS, SUBLANES, LANES), F32),
            pltpu.VMEM((tn // LANES, PEER_HEADS, 2, SUBLANES, LANES), F32),
            pltpu.VMEM((et, tn), F32),
            pltpu.VMEM((et, tn), BF16),
            pltpu.VMEM((d, tn), F32),
        ],
        compiler_params=pltpu.CompilerParams(
            dimension_semantics=("arbitrary", "arbitrary"),
            vmem_limit_bytes=VMEM_LIMIT_BYTES),
        name="peer",
    )(h, norm_w, wqt, skeys, u_bf, vt_bf)


def _group_mean_sq(x, bd):
    x2 = x * x
    hi = x2.astype(BF16)
    lo = (x2 - hi.astype(F32)).astype(BF16)
    return (jnp.dot(hi, bd, preferred_element_type=F32) + jnp.dot(lo, bd, preferred_element_type=F32))


def _head_norm(x, w, bd):
    return x * lax.rsqrt(_group_mean_sq(x, bd) + EPS) * w


def _rope(x, cos, sin_signed):
    width = x.shape[-1]
    half = HEAD_DIM // 2
    lane = lax.broadcasted_iota(jnp.int32, x.shape, 1)
    first_half = (lane % HEAD_DIM) < half
    rot = jnp.where(first_half, pltpu.roll(x, width - half, axis=1), pltpu.roll(x, half, axis=1))
    return x * cos + rot * sin_signed


def _project_in(x, nmw, win, qnw, knw, gvw, cos, sin_signed, bd):
    xn = _rms_rows(x, nmw).astype(BF16)
    z = jnp.dot(xn, win, preferred_element_type=F32)
    o1, o2, o3, o4 = ATTN_WIDTH, ATTN_WIDTH + KV_WIDTH, ATTN_WIDTH + 2 * KV_WIDTH, D_IN - GM_WIDTH
    reps = ATTN_WIDTH // LANES
    cos_q = jnp.concatenate([cos] * reps, axis=1)
    sin_q = jnp.concatenate([sin_signed] * reps, axis=1)
    q = _rope(_head_norm(z[:, :o1], qnw, bd), cos_q, sin_q) * (HEAD_DIM ** -0.5)
    k = _rope(_head_norm(z[:, o1:o2], knw, bd[:KV_WIDTH, :KV_WIDTH]), cos, sin_signed)
    v = z[:, o2:o3]
    ug = _gelu(z[:, o3:o4])
    gvn = _head_norm(_gelu(z[:, o4:]), gvw, bd)
    return q, k, v, ug, gvn


def _sink_softmax_pv(s, sink, v_bf):
    m = jnp.maximum(jnp.max(s, axis=-1, keepdims=True), sink)
    p = jnp.exp(s - m)
    denom = jnp.sum(p, axis=-1, keepdims=True) + jnp.exp(sink - m)
    return jnp.dot((p / denom).astype(BF16), v_bf, preferred_element_type=F32)


def _merge(x, attn, gm, onw, wout):
    a = _rms_rows(attn, onw[:, :ATTN_WIDTH])
    g = _rms_rows(gm, onw[:, ATTN_WIDTH:])
    cat = jnp.concatenate([a, g], axis=1).astype(BF16)
    return x + jnp.dot(cat, wout, preferred_element_type=F32)


def _mixer_prompt_kernel(sinks_ref, x_ref, nmw_ref, win_ref, qnw_ref, knw_ref, gvw_ref, cos_ref,
                         sin_ref, bd_ref, wsp_ref, bsp_ref, onw_ref, wout_ref,
                         h_ref, kout_ref, vout_ref,
                         q_ref, kext_ref, vext_ref, ug_ref, gvn_ref, attn_ref, gm_ref, wtril_ref,
                         *, tb):
    i = pl.program_id(0)
    nsb = tb // WINDOW

    @pl.when(i == 0)
    def _init():
        kext_ref[0:WINDOW, :] = jnp.zeros((WINDOW, KV_WIDTH), BF16)
        vext_ref[0:WINDOW, :] = jnp.zeros((WINDOW, KV_WIDTH), BF16)
        row = lax.broadcasted_iota(jnp.int32, (CHUNK, CHUNK), 0)
        col = lax.broadcasted_iota(jnp.int32, (CHUNK, CHUNK), 1)
        for hd in range(GM_HEADS):
            wtril_ref[hd] = jnp.where(row >= col, wsp_ref[hd], 0.0).astype(BF16)

    x = x_ref[...]
    q, k, v, ug, gvn = _project_in(x, nmw_ref[...], win_ref[...], qnw_ref[...], knw_ref[...],
                                   gvw_ref[...], cos_ref[...], sin_ref[...], bd_ref[...])
    q_ref[...] = q.astype(BF16)
    kext_ref[WINDOW:, :] = k.astype(BF16)
    vext_ref[WINDOW:, :] = v.astype(BF16)
    ug_ref[...] = ug
    gvn_ref[...] = gvn.astype(BF16)
    kout_ref[...] = k[tb - WINDOW:, :]
    vout_ref[...] = v[tb - WINDOW:, :]

    qi = lax.broadcasted_iota(jnp.int32, (WINDOW, 2 * WINDOW), 0)
    kj = lax.broadcasted_iota(jnp.int32, (WINDOW, 2 * WINDOW), 1)
    band = (kj > qi) & (kj <= qi + WINDOW)
    lane = lax.broadcasted_iota(jnp.int32, (CHUNK, LANES), 1)

    def sub_block(b, carry):
        r0 = pl.multiple_of(b * WINDOW, WINDOW)
        rows = pl.ds(r0, WINDOW)
        kb = kext_ref[pl.ds(r0, 2 * WINDOW), :]
        vb = vext_ref[pl.ds(r0, 2 * WINDOW), :]
        first = jnp.logical_and(i == 0, b == 0)
        mask = band & (kj >= jnp.where(first, WINDOW, 0))
        qb = q_ref[rows, :]
        outs = []
        for hd in range(ATTN_HEADS):
            g = hd // GQA_GROUP
            qh = qb[:, hd * HEAD_DIM:(hd + 1) * HEAD_DIM]
            kg = kb[:, g * HEAD_DIM:(g + 1) * HEAD_DIM]
            vg = vb[:, g * HEAD_DIM:(g + 1) * HEAD_DIM]
            s = lax.dot_general(qh, kg, (((1,), (1,)), ((), ())), preferred_element_type=F32)
            s = jnp.where(mask, s, NEG_INF)
            outs.append(_sink_softmax_pv(s, sinks_ref[hd], vg))
        attn_ref[rows, :] = jnp.concatenate(outs, axis=1)
        gvb = gvn_ref[rows, :]
        tiles = []
        for t in range(GM_WIDTH // LANES):
            gt = gvb[:, t * LANES:(t + 1) * LANES]
            y0 = jnp.dot(wtril_ref[2 * t], gt, preferred_element_type=F32)
            y1 = jnp.dot(wtril_ref[2 * t + 1], gt, preferred_element_type=F32)
            tiles.append(jnp.where(lane < GM_HEAD_DIM, y0, y1))
        mixed = jnp.concatenate(tiles, axis=1) + bsp_ref[...]
        gm_ref[rows, :] = ug_ref[rows, :] * mixed
        return carry

    lax.fori_loop(0, nsb, sub_block, 0)

    h_ref[...] = _merge(x, attn_ref[...], gm_ref[...], onw_ref[...], wout_ref[...])
    kext_ref[0:WINDOW, :] = kext_ref[tb:tb + WINDOW, :]
    vext_ref[0:WINDOW, :] = vext_ref[tb:tb + WINDOW, :]


def _const_spec(shape):
    nd = len(shape)
    return pl.BlockSpec(shape, lambda i: (0,) * nd)


def _mixer_prompt_call(x, sinks, nmw, win, qnw, knw, gvw, cos, sin, bd, wsp, bsp, onw, wout, *, tb):
    n, d = x.shape
    assert n % tb == 0 and tb % WINDOW == 0
    kern = functools.partial(_mixer_prompt_kernel, tb=tb)
    return pl.pallas_call(
        kern,
        out_shape=(jax.ShapeDtypeStruct((n, d), F32),
                   jax.ShapeDtypeStruct((WINDOW, KV_WIDTH), F32),
                   jax.ShapeDtypeStruct((WINDOW, KV_WIDTH), F32)),
        grid=(n // tb,),
        in_specs=[
            pl.BlockSpec(memory_space=pltpu.SMEM),
            pl.BlockSpec((tb, d), lambda i: (i, 0)),
            _const_spec(nmw.shape), _const_spec(win.shape), _const_spec(qnw.shape),
            _const_spec(knw.shape), _const_spec(gvw.shape),
            pl.BlockSpec((tb, LANES), lambda i: (i, 0)),
            pl.BlockSpec((tb, LANES), lambda i: (i, 0)),
            _const_spec(bd.shape), _const_spec(wsp.shape), _const_spec(bsp.shape),
            _const_spec(onw.shape), _const_spec(wout.shape),
        ],
        out_specs=(pl.BlockSpec((tb, d), lambda i: (i, 0)),
                   _const_spec((WINDOW, KV_WIDTH)), _const_spec((WINDOW, KV_WIDTH))),
        scratch_shapes=[
            pltpu.VMEM((tb, ATTN_WIDTH), BF16),
            pltpu.VMEM((tb + WINDOW, KV_WIDTH), BF16),
            pltpu.VMEM((tb + WINDOW, KV_WIDTH), BF16),
            pltpu.VMEM((tb, GM_WIDTH), F32),
            pltpu.VMEM((tb, GM_WIDTH), BF16),
            pltpu.VMEM((tb, ATTN_WIDTH), F32),
            pltpu.VMEM((tb, GM_WIDTH), F32),
            pltpu.VMEM((GM_HEADS, CHUNK, CHUNK), BF16),
        ],
        compiler_params=pltpu.CompilerParams(
            dimension_semantics=("arbitrary",), vmem_limit_bytes=VMEM_LIMIT_BYTES),
        name="mixer_prompt",
    )(sinks, x, nmw, win, qnw, knw, gvw, cos, sin, bd, wsp, bsp, onw, wout)


SAMPLE_CHUNK = 8


def _mixer_sample_kernel(sinks_ref, x_ref, ck_ref, cv_ref, nmw_ref, win_ref, qnw_ref, knw_ref,
                         gvw_ref, cos_ref, sin_ref, bd_ref, w00_ref, b0_ref, onw_ref, wout_ref,
                         h_ref, kout_ref, vout_ref, gvout_ref, attn_ref):
    nb = x_ref.shape[0]
    x = x_ref[...]
    q, k, v, ug, gvn = _project_in(x, nmw_ref[...], win_ref[...], qnw_ref[...], knw_ref[...],
                                   gvw_ref[...], cos_ref[...], sin_ref[...], bd_ref[...])
    kout_ref[...] = k
    vout_ref[...] = v
    gvout_ref[...] = gvn

    cb = SAMPLE_CHUNK
    lane = lax.broadcasted_iota(jnp.int32, (cb, LANES), 1)
    jrow = lax.broadcasted_iota(jnp.int32, (WINDOW, LANES), 0)
    rseq = lax.broadcasted_iota(jnp.int32, (GQA_GROUP * cb, cb * WINDOW), 0) % cb
    cseq = lax.broadcasted_iota(jnp.int32, (GQA_GROUP * cb, cb * WINDOW), 1) // WINDOW
    same_seq = rseq == cseq
    for c in range(nb // cb):
        rows = slice(c * cb, (c + 1) * cb)
        k_eff = jnp.concatenate(
            [jnp.where(jrow == 0, k[b:b + 1, :], ck_ref[b]) for b in range(c * cb, (c + 1) * cb)],
            axis=0).astype(BF16)
        v_eff = jnp.concatenate(
            [jnp.where(jrow == 0, v[b:b + 1, :], cv_ref[b]) for b in range(c * cb, (c + 1) * cb)],
            axis=0).astype(BF16)
        outs = [None] * ATTN_HEADS
        for g in range(KV_HEADS):
            qz = []
            for hh in range(GQA_GROUP):
                hd = g * GQA_GROUP + hh
                tile = q[rows, (hd // 2) * LANES:(hd // 2 + 1) * LANES]
                if hd % 2 != g:
                    tile = pltpu.roll(tile, HEAD_DIM, axis=1)
                qz.append(jnp.where((lane // HEAD_DIM) == g, tile, 0.0))
            qz = jnp.concatenate(qz, axis=0).astype(BF16)
            s = lax.dot_general(qz, k_eff, (((1,), (1,)), ((), ())), preferred_element_type=F32)
            s = jnp.where(same_seq, s, NEG_INF)
            for hh in range(GQA_GROUP):
                hd = g * GQA_GROUP + hh
                o = _sink_softmax_pv(s[hh * cb:(hh + 1) * cb, :], sinks_ref[hd], v_eff)
                outs[hd] = o[:, g * HEAD_DIM:(g + 1) * HEAD_DIM]
        attn_ref[rows, :] = jnp.concatenate(outs, axis=1)

    gm = ug * (w00_ref[...] * gvn + b0_ref[...])
    h_ref[...] = _merge(x, attn_ref[...], gm, onw_ref[...], wout_ref[...])


def _mixer_sample_call(x, sinks, ck, cv, nmw, win, qnw, knw, gvw, cos, sin, bd, w00, b0, onw, wout):
    nb, d = x.shape
    assert nb % SAMPLE_CHUNK == 0
    vmem = pl.BlockSpec(memory_space=pltpu.VMEM)
    return pl.pallas_call(
        _mixer_sample_kernel,
        out_shape=(jax.ShapeDtypeStruct((nb, d), F32),
                   jax.ShapeDtypeStruct((nb, KV_WIDTH), F32),
                   jax.ShapeDtypeStruct((nb, KV_WIDTH), F32),
                   jax.ShapeDtypeStruct((nb, GM_WIDTH), F32)),
        in_specs=[pl.BlockSpec(memory_space=pltpu.SMEM)] + [vmem] * 15,
        out_specs=(vmem, vmem, vmem, vmem),
        scratch_shapes=[pltpu.VMEM((nb, ATTN_WIDTH), F32)],
        compiler_params=pltpu.CompilerParams(vmem_limit_bytes=VMEM_LIMIT_BYTES),
        name="mixer_sample",
    )(sinks, x, ck, cv, nmw, win, qnw, knw, gvw, cos, sin, bd, w00, b0, onw, wout)


def _rope_tables(pos):
    half = HEAD_DIM // 2
    inv = ROPE_THETA ** (-jnp.arange(half, dtype=F32) / half)
    ang = pos.astype(F32)[:, None] * inv[None, :]
    cos, sin = jnp.cos(ang), jnp.sin(ang)
    return (jnp.concatenate([cos, cos, cos, cos], axis=1),
            jnp.concatenate([-sin, sin, -sin, sin], axis=1))


PROMPT_BLOCK = 512
PEER_TOKENS = 512
PEER_EXPERT_TILE = 1024


def kernel(x_prompt, x_sample, cache_k, cache_v, norm_mix_w, w_in, q_norm_w, k_norm_w, sinks,
           gm_v_norm_w, w_spatial, b_spatial, out_norm_w, w_out, norm_ffn_w, w_query, sub_keys,
           expert_u, expert_v):
    depth, batch, seq = w_in.shape[0], x_prompt.shape[0], x_prompt.shape[1]
    dec_batch, dec_seq = x_sample.shape[0], x_sample.shape[1]
    assert depth == 1 and batch == 1 and dec_seq == 1
    past_len = seq
    layer = 0

    nmw = norm_mix_w[layer][None, :]
    win = w_in[layer].astype(BF16)
    qnw = jnp.tile(q_norm_w[layer], ATTN_HEADS)[None, :]
    knw = jnp.tile(k_norm_w[layer], KV_HEADS)[None, :]
    gvw = gm_v_norm_w[layer].reshape(1, GM_WIDTH)
    group = jnp.arange(ATTN_WIDTH) // HEAD_DIM
    bd = jnp.where(group[:, None] == group[None, :], 1.0 / HEAD_DIM, 0.0).astype(BF16)
    wsp = w_spatial[layer]
    bsp = jnp.repeat(b_spatial[layer].T, GM_HEAD_DIM, axis=1)
    w00 = jnp.repeat(w_spatial[layer][:, 0, 0], GM_HEAD_DIM)[None, :]
    b0 = jnp.repeat(b_spatial[layer][:, 0], GM_HEAD_DIM)[None, :]
    onw = out_norm_w[layer][None, :]
    wout = w_out[layer].astype(BF16)
    nfw = norm_ffn_w[layer][None, :]
    wqt = w_query[layer].T.astype(BF16)
    skeys = sub_keys[layer].reshape(PEER_HEADS * 2, N_KEYS, HALF_KEY).astype(BF16)
    u_bf = expert_u[layer].astype(BF16)
    vt_bf = expert_v[layer].T.astype(BF16)
    sinks_l = sinks[layer]
    cos_p, sin_p = _rope_tables(jnp.arange(seq, dtype=jnp.int32))
    cos_s, sin_s = _rope_tables(past_len + jnp.zeros((dec_batch,), jnp.int32))

    h_p, nk_p, nv_p = _mixer_prompt_call(x_prompt[0], sinks_l, nmw, win, qnw, knw, gvw, cos_p, sin_p,
                                         bd, wsp, bsp, onw, wout, tb=PROMPT_BLOCK)
    y_p = _peer_call(h_p, nfw, wqt, skeys, u_bf, vt_bf, tn=PEER_TOKENS, et=PEER_EXPERT_TILE)

    ck = cache_k[layer].reshape(dec_batch, WINDOW, KV_WIDTH)
    cv = cache_v[layer].reshape(dec_batch, WINDOW, KV_WIDTH)
    h_s, k_s, v_s, gv_s = _mixer_sample_call(x_sample[:, 0], sinks_l, ck, cv, nmw, win, qnw, knw, gvw,
                                             cos_s, sin_s, bd, w00, b0, onw, wout)
    y_s = _peer_call(h_s, nfw, wqt, skeys, u_bf, vt_bf, tn=dec_batch, et=PEER_EXPERT_TILE)

    nk_s = jnp.concatenate([ck[:, 1:], k_s[:, None, :]], axis=1)
    nv_s = jnp.concatenate([cv[:, 1:], v_s[:, None, :]], axis=1)
    return (y_p[None],
            y_s[:, None, :],
            nk_p.reshape(1, 1, WINDOW, KV_HEADS, HEAD_DIM),
            nv_p.reshape(1, 1, WINDOW, KV_HEADS, HEAD_DIM),
            nk_s.reshape(1, dec_batch, WINDOW, KV_HEADS, HEAD_DIM),
            nv_s.reshape(1, dec_batch, WINDOW, KV_HEADS, HEAD_DIM),
            gv_s.reshape(1, dec_batch, 1, GM_HEADS, GM_HEAD_DIM))
```

```python
import functools

import jax
import jax.numpy as jnp
import numpy as np
from jax import lax
from jax.experimental import pallas as pl
from jax.experimental.pallas import tpu as pltpu

F32 = jnp.float32
BF16 = jnp.bfloat16

D_MODEL = 1024
HEAD_DIM = 64
ATTN_HEADS = 8
KV_HEADS = 2
GQA_GROUP = ATTN_HEADS // KV_HEADS
WINDOW = 128
ROPE_THETA = 10000.0
GM_HEADS = 8
GM_HEAD_DIM = 64
CHUNK = 128
ATTN_WIDTH = ATTN_HEADS * HEAD_DIM
KV_WIDTH = KV_HEADS * HEAD_DIM
GM_WIDTH = GM_HEADS * GM_HEAD_DIM
D_MIX = ATTN_WIDTH + GM_WIDTH
D_IN = ATTN_WIDTH + 2 * KV_WIDTH + 2 * GM_WIDTH
PEER_HEADS = 8
N_KEYS = 128
N_EXPERTS = N_KEYS * N_KEYS
D_KEY = 256
HALF_KEY = D_KEY // 2
TOPK = 16
EPS = 1e-6
NEG_INF = -1e30

LANES = 128
SUBLANES = 8
VMEM_LIMIT_BYTES = 56 * 1024 * 1024


def _gelu(x):
    return 0.5 * x * (1.0 + lax.erf(x * np.float32(np.sqrt(0.5))))


def _rms_rows(x, w):
    ms = jnp.mean(x * x, axis=-1, keepdims=True)
    return x * lax.rsqrt(ms + EPS) * w


def _bitonic_merge_desc(xs):
    xs = list(xs)
    n = len(xs)
    j = n // 2
    while j >= 1:
        for i in range(n):
            l = i ^ j
            if l > i:
                a, b = xs[i], xs[l]
                xs[i] = jnp.maximum(a, b)
                xs[l] = jnp.minimum(a, b)
        j //= 2
    return xs


def _bitonic_sort_desc(xs):
    xs = list(xs)
    n = len(xs)
    k = 2
    while k <= n:
        j = k // 2
        while j >= 1:
            for i in range(n):
                l = i ^ j
                if l > i:
                    a, b = xs[i], xs[l]
                    hi, lo = jnp.maximum(a, b), jnp.minimum(a, b)
                    if (i & k) == 0:
                        xs[i], xs[l] = hi, lo
                    else:
                        xs[i], xs[l] = lo, hi
            j //= 2
        k *= 2
    return xs


def _top16_rows_desc(s):
    xs = _bitonic_sort_desc([s[SUBLANES * r:SUBLANES * (r + 1), :] for r in range(TOPK)])
    for shift in (4, 2, 1):
        other = [pltpu.roll(x, shift, axis=0) for x in xs]
        xs = _bitonic_merge_desc([jnp.maximum(xs[i], other[TOPK - 1 - i]) for i in range(TOPK)])
    return xs


_CAND_PAIRS = [(i, j) for i in range(TOPK) for j in range(TOPK) if (i + 1) * (j + 1) <= TOPK]


def _peer_kernel(h_ref, nw_ref, wqt_ref, sk_ref, u_ref, vt_ref, y_ref,
                 hnT_ref, qT_ref, s1_ref, s2_ref, e1_ref, se_ref, thr_ref, rows_ref, a_ref, g_ref,
                 acc_ref,
                 *, tn, et, strip):
    j = pl.program_id(1)
    n_lt = tn // LANES
    _dense_tile = functools.partial(_peer_dense_tile, se_ref=se_ref, thr_ref=thr_ref,
                                    rows_ref=rows_ref, tn=tn, et=et, strip=strip)

    @pl.when(j == 0)
    def _prologue():
        hn = _rms_rows(h_ref[...], nw_ref[...])
        hnT = hn.T.astype(BF16)
        hnT_ref[...] = hnT
        qT_ref[...] = jnp.dot(wqt_ref[...], hnT, preferred_element_type=F32).astype(BF16)
        sub = lax.broadcasted_iota(jnp.int32, (SUBLANES, tn), 0)

        def head_body(hd, tops):
            tops = list(tops)
            for p, s_ref in enumerate((s1_ref, s2_ref)):
                c0 = pl.multiple_of((hd * 2 + p) * HALF_KEY, HALF_KEY)
                q = qT_ref[pl.ds(c0, HALF_KEY), :]
                s = jnp.dot(sk_ref[hd * 2 + p], q, preferred_element_type=F32)
                s_ref[hd] = s
                ranked = _top16_rows_desc(s)
                for r in range(TOPK):
                    tops[p * TOPK + r] = jnp.where(sub == hd, ranked[r], tops[p * TOPK + r])
            return tuple(tops)

        tops = lax.fori_loop(0, PEER_HEADS, head_body,
                             tuple(jnp.zeros((SUBLANES, tn), F32) for _ in range(2 * TOPK)))
        top1, top2 = tops[:TOPK], tops[TOPK:]
        cands = [top1[i] + top2[jj] for (i, jj) in _CAND_PAIRS]
        cands += [jnp.full((SUBLANES, tn), -jnp.inf, F32)] * (64 - len(cands))
        best = _bitonic_sort_desc(cands)[:TOPK]
        m = best[0]
        z = jnp.exp(best[0] - m)
        for r in range(1, TOPK):
            z = z + jnp.exp(best[r] - m)
        thr = best[TOPK - 1]
        inv_z = 1.0 / z
        n_rg = N_KEYS // SUBLANES
        for hd in range(PEER_HEADS):
            e1_ref[hd] = jnp.exp(s1_ref[hd] - top1[0][hd:hd + 1, :])
            s2 = s2_ref[hd]
            e2 = jnp.exp(s2 - top2[0][hd:hd + 1, :]) * inv_z[hd:hd + 1, :]
            for t in range(n_lt):
                lanes = slice(t * LANES, (t + 1) * LANES)
                se_ref[hd, t, :, 0] = s2[:, lanes].reshape(n_rg, SUBLANES, LANES)
                se_ref[hd, t, :, 1] = e2[:, lanes].reshape(n_rg, SUBLANES, LANES)
                thr_ref[t, hd] = jnp.broadcast_to(thr[hd:hd + 1, lanes], (SUBLANES, LANES))
        acc_ref[...] = jnp.zeros_like(acc_ref)

    _peer_first_key_rows(j, s1_ref, e1_ref, rows_ref, tn=tn, et=et)
    a_ref[...] = jnp.dot(u_ref[...], hnT_ref[...], preferred_element_type=F32)
    _dense_tile(a_ref, g_ref)
    acc_ref[...] += jnp.dot(vt_ref[...], g_ref[...], preferred_element_type=F32)

    @pl.when(j == pl.num_programs(1) - 1)
    def _epilogue():
        y_ref[...] = h_ref[...] + acc_ref[...].T


def _peer_first_key_rows(j, s1_ref, e1_ref, rows_ref, *, tn, et):
    n_sub = et // N_KEYS
    n_lt = tn // LANES
    i1_base = pl.multiple_of(j * n_sub, SUBLANES)
    for g in range(n_sub // SUBLANES):
        g8 = pl.ds(i1_base + g * SUBLANES, SUBLANES)
        for hd in range(PEER_HEADS):
            s1g = s1_ref[hd, g8, :]
            e1g = e1_ref[hd, g8, :]
            for o in range(SUBLANES):
                for t in range(n_lt):
                    lanes = slice(t * LANES, (t + 1) * LANES)
                    r = g * SUBLANES + o
                    rows_ref[r, t, hd, 0] = jnp.broadcast_to(s1g[o:o + 1, lanes], (SUBLANES, LANES))
                    rows_ref[r, t, hd, 1] = jnp.broadcast_to(e1g[o:o + 1, lanes], (SUBLANES, LANES))


def _peer_dense_tile(a_ref, g_ref, se_ref, thr_ref, rows_ref, *, tn, et, strip):
    n_sub = et // N_KEYS
    n_lt = tn // LANES
    for r in range(n_sub):

        def strip_body(sidx, carry, r=r):
            row0 = sidx * strip
            erow = r * N_KEYS + sidx * strip
            for t in range(n_lt):
                lanes = slice(t * LANES, (t + 1) * LANES)
                ws = [jnp.zeros((SUBLANES, LANES), F32)] * (strip // SUBLANES)
                for hd in range(PEER_HEADS):
                    s1row = rows_ref[r, t, hd, 0]
                    e1row = rows_ref[r, t, hd, 1]
                    th = thr_ref[t, hd]
                    for v in range(strip // SUBLANES):
                        rg = row0 // SUBLANES + v
                        s2 = se_ref[hd, t, rg, 0]
                        e2 = se_ref[hd, t, rg, 1]
                        ws[v] = ws[v] + jnp.where(s2 + s1row >= th, e2 * e1row, 0.0)
                w = jnp.concatenate(ws, axis=0)
                a = a_ref[pl.ds(erow, strip), lanes]
                g_ref[pl.ds(erow, strip), lanes] = (w * _gelu(a)).astype(BF16)
            return carry

        for sidx in range(N_KEYS // strip):
            strip_body(sidx, 0)


def _peer_call(h, norm_w, wqt, skeys, u_bf, vt_bf, *, tn, et, strip=32):
    n, d = h.shape
    n_exp = u_bf.shape[0]
    assert n % tn == 0 and n_exp % et == 0 and et % (SUBLANES * N_KEYS) == 0 and tn % LANES == 0
    grid = (n // tn, n_exp // et)
    kern = functools.partial(_peer_kernel, tn=tn, et=et, strip=strip)
    table = pltpu.VMEM((PEER_HEADS, N_KEYS, tn), F32)
    return pl.pallas_call(
        kern,
        out_shape=jax.ShapeDtypeStruct((n, d), F32),
        grid=grid,
        in_specs=[
            pl.BlockSpec((tn, d), lambda i, j: (i, 0)),
            pl.BlockSpec((1, d), lambda i, j: (0, 0)),
            pl.BlockSpec(wqt.shape, lambda i, j: (0, 0)),
            pl.BlockSpec(skeys.shape, lambda i, j: (0, 0, 0)),
            pl.BlockSpec((et, d), lambda i, j: (j, 0)),
            pl.BlockSpec((d, et), lambda i, j: (0, j)),
        ],
        out_specs=pl.BlockSpec((tn, d), lambda i, j: (i, 0)),
        scratch_shapes=[
            pltpu.VMEM((d, tn), BF16),
            pltpu.VMEM((wqt.shape[0], tn), BF16),
            table, table, table,
            pltpu.VMEM((PEER_HEADS, tn // LANES, N_KEYS // SUBLANES, 2, SUBLANES, LANES), F32),
            pltpu.VMEM((tn // LANES, PEER_HEADS, SUBLANES, LANES), F32),
            pltpu.VMEM((et // N_KEYS, tn // LANES, PEER_HEADS, 2, SUBLANES, LANES), F32),
            pltpu.VMEM((et, tn), F32),
            pltpu.VMEM((et, tn), BF16),
            pltpu.VMEM((d, tn), F32),
        ],
        compiler_params=pltpu.CompilerParams(
            dimension_semantics=("arbitrary", "arbitrary"),
            vmem_limit_bytes=VMEM_LIMIT_BYTES),
        name="peer",
    )(h, norm_w, wqt, skeys, u_bf, vt_bf)


def _group_mean_sq(x, bd):
    x2 = x * x
    hi = x2.astype(BF16)
    lo = (x2 - hi.astype(F32)).astype(BF16)
    return (jnp.dot(hi, bd, preferred_element_type=F32) + jnp.dot(lo, bd, preferred_element_type=F32))


def _head_norm(x, w, bd):
    return x * lax.rsqrt(_group_mean_sq(x, bd) + EPS) * w


def _rope(x, cos, sin_signed):
    width = x.shape[-1]
    half = HEAD_DIM // 2
    lane = lax.broadcasted_iota(jnp.int32, x.shape, 1)
    first_half = (lane % HEAD_DIM) < half
    rot = jnp.where(first_half, pltpu.roll(x, width - half, axis=1), pltpu.roll(x, half, axis=1))
    return x * cos + rot * sin_signed


def _project_in(x, nmw, win, qnw, knw, gvw, cos, sin_signed, bd):
    xn = _rms_rows(x, nmw).astype(BF16)
    z = jnp.dot(xn, win, preferred_element_type=F32)
    o1, o2, o3, o4 = ATTN_WIDTH, ATTN_WIDTH + KV_WIDTH, ATTN_WIDTH + 2 * KV_WIDTH, D_IN - GM_WIDTH
    reps = ATTN_WIDTH // LANES
    cos_q = jnp.concatenate([cos] * reps, axis=1)
    sin_q = jnp.concatenate([sin_signed] * reps, axis=1)
    q = _rope(_head_norm(z[:, :o1], qnw, bd), cos_q, sin_q) * (HEAD_DIM ** -0.5)
    k = _rope(_head_norm(z[:, o1:o2], knw, bd[:KV_WIDTH, :KV_WIDTH]), cos, sin_signed)
    v = z[:, o2:o3]
    ug = _gelu(z[:, o3:o4])
    gvn = _head_norm(_gelu(z[:, o4:]), gvw, bd)
    return q, k, v, ug, gvn


def _sink_softmax_pv(s, sink, v_bf):
    m = jnp.maximum(jnp.max(s, axis=-1, keepdims=True), sink)
    p = jnp.exp(s - m)
    denom = jnp.sum(p, axis=-1, keepdims=True) + jnp.exp(sink - m)
    return jnp.dot((p / denom).astype(BF16), v_bf, preferred_element_type=F32)


def _merge(x, attn, gm, onw, wout):
    a = _rms_rows(attn, onw[:, :ATTN_WIDTH])
    g = _rms_rows(gm, onw[:, ATTN_WIDTH:])
    cat = jnp.concatenate([a, g], axis=1).astype(BF16)
    return x + jnp.dot(cat, wout, preferred_element_type=F32)


def _mixer_prompt_kernel(sinks_ref, x_ref, nmw_ref, win_ref, qnw_ref, knw_ref, gvw_ref, cos_ref,
                         sin_ref, bd_ref, wsp_ref, bsp_ref, onw_ref, wout_ref,
                         h_ref, kout_ref, vout_ref,
                         q_ref, kext_ref, vext_ref, ug_ref, gvn_ref, attn_ref, gm_ref, wtril_ref,
                         *, tb):
    i = pl.program_id(0)
    nsb = tb // WINDOW

    @pl.when(i == 0)
    def _init():
        kext_ref[0:WINDOW, :] = jnp.zeros((WINDOW, KV_WIDTH), BF16)
        vext_ref[0:WINDOW, :] = jnp.zeros((WINDOW, KV_WIDTH), BF16)
        row = lax.broadcasted_iota(jnp.int32, (CHUNK, CHUNK), 0)
        col = lax.broadcasted_iota(jnp.int32, (CHUNK, CHUNK), 1)
        for hd in range(GM_HEADS):
            wtril_ref[hd] = jnp.where(row >= col, wsp_ref[hd], 0.0).astype(BF16)

    x = x_ref[...]
    q, k, v, ug, gvn = _project_in(x, nmw_ref[...], win_ref[...], qnw_ref[...], knw_ref[...],
                                   gvw_ref[...], cos_ref[...], sin_ref[...], bd_ref[...])
    q_ref[...] = q.astype(BF16)
    kext_ref[WINDOW:, :] = k.astype(BF16)
    vext_ref[WINDOW:, :] = v.astype(BF16)
    ug_ref[...] = ug
    gvn_ref[...] = gvn.astype(BF16)
    kout_ref[...] = k[tb - WINDOW:, :]
    vout_ref[...] = v[tb - WINDOW:, :]

    qi = lax.broadcasted_iota(jnp.int32, (WINDOW, 2 * WINDOW), 0)
    kj = lax.broadcasted_iota(jnp.int32, (WINDOW, 2 * WINDOW), 1)
    band = (kj > qi) & (kj <= qi + WINDOW)
    lane = lax.broadcasted_iota(jnp.int32, (CHUNK, LANES), 1)

    def sub_block(b, carry):
        r0 = b * WINDOW
        rows = pl.ds(r0, WINDOW)
        kb = kext_ref[pl.ds(r0, 2 * WINDOW), :]
        vb = vext_ref[pl.ds(r0, 2 * WINDOW), :]
        first = jnp.logical_and(i == 0, b == 0)
        mask = band & (kj >= jnp.where(first, WINDOW, 0))
        qb = q_ref[rows, :]
        outs = []
        for hd in range(ATTN_HEADS):
            g = hd // GQA_GROUP
            qh = qb[:, hd * HEAD_DIM:(hd + 1) * HEAD_DIM]
            kg = kb[:, g * HEAD_DIM:(g + 1) * HEAD_DIM]
            vg = vb[:, g * HEAD_DIM:(g + 1) * HEAD_DIM]
            s = lax.dot_general(qh, kg, (((1,), (1,)), ((), ())), preferred_element_type=F32)
            s = jnp.where(mask, s, NEG_INF)
            outs.append(_sink_softmax_pv(s, sinks_ref[hd], vg))
        attn_ref[rows, :] = jnp.concatenate(outs, axis=1)
        gvb = gvn_ref[rows, :]
        tiles = []
        for t in range(GM_WIDTH // LANES):
            gt = gvb[:, t * LANES:(t + 1) * LANES]
            y0 = jnp.dot(wtril_ref[2 * t], gt, preferred_element_type=F32)
            y1 = jnp.dot(wtril_ref[2 * t + 1], gt, preferred_element_type=F32)
            tiles.append(jnp.where(lane < GM_HEAD_DIM, y0, y1))
        mixed = jnp.concatenate(tiles, axis=1) + bsp_ref[...]
        gm_ref[rows, :] = ug_ref[rows, :] * mixed
        return carry

    for b in range(nsb):
        sub_block(b, 0)

    h_ref[...] = _merge(x, attn_ref[...], gm_ref[...], onw_ref[...], wout_ref[...])
    kext_ref[0:WINDOW, :] = kext_ref[tb:tb + WINDOW, :]
    vext_ref[0:WINDOW, :] = vext_ref[tb:tb + WINDOW, :]


def _const_spec(shape):
    nd = len(shape)
    return pl.BlockSpec(shape, lambda i: (0,) * nd)


def _mixer_prompt_call(x, sinks, nmw, win, qnw, knw, gvw, cos, sin, bd, wsp, bsp, onw, wout, *, tb):
    n, d = x.shape
    assert n % tb == 0 and tb % WINDOW == 0
    kern = functools.partial(_mixer_prompt_kernel, tb=tb)
    return pl.pallas_call(
        kern,
        out_shape=(jax.ShapeDtypeStruct((n, d), F32),
                   jax.ShapeDtypeStruct((WINDOW, KV_WIDTH), F32),
                   jax.ShapeDtypeStruct((WINDOW, KV_WIDTH), F32)),
        grid=(n // tb,),
        in_specs=[
            pl.BlockSpec(memory_space=pltpu.SMEM),
            pl.BlockSpec((tb, d), lambda i: (i, 0)),
            _const_spec(nmw.shape), _const_spec(win.shape), _const_spec(qnw.shape),
            _const_spec(knw.shape), _const_spec(gvw.shape),
            pl.BlockSpec((tb, LANES), lambda i: (i, 0)),
            pl.BlockSpec((tb, LANES), lambda i: (i, 0)),
            _const_spec(bd.shape), _const_spec(wsp.shape), _const_spec(bsp.shape),
            _const_spec(onw.shape), _const_spec(wout.shape),
        ],
        out_specs=(pl.BlockSpec((tb, d), lambda i: (i, 0)),
                   _const_spec((WINDOW, KV_WIDTH)), _const_spec((WINDOW, KV_WIDTH))),
        scratch_shapes=[
            pltpu.VMEM((tb, ATTN_WIDTH), BF16),
            pltpu.VMEM((tb + WINDOW, KV_WIDTH), BF16),
            pltpu.VMEM((tb + WINDOW, KV_WIDTH), BF16),
            pltpu.VMEM((tb, GM_WIDTH), F32),
            pltpu.VMEM((tb, GM_WIDTH), BF16),
            pltpu.VMEM((tb, ATTN_WIDTH), F32),
            pltpu.VMEM((tb, GM_WIDTH), F32),
            pltpu.VMEM((GM_HEADS, CHUNK, CHUNK), BF16),
        ],
        compiler_params=pltpu.CompilerParams(
            dimension_semantics=("arbitrary",), vmem_limit_bytes=VMEM_LIMIT_BYTES),
        name="mixer_prompt",
    )(sinks, x, nmw, win, qnw, knw, gvw, cos, sin, bd, wsp, bsp, onw, wout)


SAMPLE_CHUNK = 8


def _mixer_sample_kernel(sinks_ref, x_ref, ck_ref, cv_ref, nmw_ref, win_ref, qnw_ref, knw_ref,
                         gvw_ref, cos_ref, sin_ref, bd_ref, w00_ref, b0_ref, onw_ref, wout_ref,
                         h_ref, kout_ref, vout_ref, gvout_ref, attn_ref):
    nb = x_ref.shape[0]
    x = x_ref[...]
    q, k, v, ug, gvn = _project_in(x, nmw_ref[...], win_ref[...], qnw_ref[...], knw_ref[...],
                                   gvw_ref[...], cos_ref[...], sin_ref[...], bd_ref[...])
    kout_ref[...] = k
    vout_ref[...] = v
    gvout_ref[...] = gvn

    cb = SAMPLE_CHUNK
    lane = lax.broadcasted_iota(jnp.int32, (cb, LANES), 1)
    jrow = lax.broadcasted_iota(jnp.int32, (WINDOW, LANES), 0)
    rseq = lax.broadcasted_iota(jnp.int32, (GQA_GROUP * cb, cb * WINDOW), 0) % cb
    cseq = lax.broadcasted_iota(jnp.int32, (GQA_GROUP * cb, cb * WINDOW), 1) // WINDOW
    same_seq = rseq == cseq
    for c in range(nb // cb):
        rows = slice(c * cb, (c + 1) * cb)
        k_eff = jnp.concatenate(
            [jnp.where(jrow == 0, k[b:b + 1, :], ck_ref[b]) for b in range(c * cb, (c + 1) * cb)],
            axis=0).astype(BF16)
        v_eff = jnp.concatenate(
            [jnp.where(jrow == 0, v[b:b + 1, :], cv_ref[b]) for b in range(c * cb, (c + 1) * cb)],
            axis=0).astype(BF16)
        outs = [None] * ATTN_HEADS
        for g in range(KV_HEADS):
            qz = []
            for hh in range(GQA_GROUP):
                hd = g * GQA_GROUP + hh
                tile = q[rows, (hd // 2) * LANES:(hd // 2 + 1) * LANES]
                if hd % 2 != g:
                    tile = pltpu.roll(tile, HEAD_DIM, axis=1)
                qz.append(jnp.where((lane // HEAD_DIM) == g, tile, 0.0))
            qz = jnp.concatenate(qz, axis=0).astype(BF16)
            s = lax.dot_general(qz, k_eff, (((1,), (1,)), ((), ())), preferred_element_type=F32)
            s = jnp.where(same_seq, s, NEG_INF)
            for hh in range(GQA_GROUP):
                hd = g * GQA_GROUP + hh
                o = _sink_softmax_pv(s[hh * cb:(hh + 1) * cb, :], sinks_ref[hd], v_eff)
                outs[hd] = o[:, g * HEAD_DIM:(g + 1) * HEAD_DIM]
        attn_ref[rows, :] = jnp.concatenate(outs, axis=1)

    gm = ug * (w00_ref[...] * gvn + b0_ref[...])
    h_ref[...] = _merge(x, attn_ref[...], gm, onw_ref[...], wout_ref[...])


def _mixer_sample_call(x, sinks, ck, cv, nmw, win, qnw, knw, gvw, cos, sin, bd, w00, b0, onw, wout):
    nb, d = x.shape
    assert nb % SAMPLE_CHUNK == 0
    vmem = pl.BlockSpec(memory_space=pltpu.VMEM)
    return pl.pallas_call(
        _mixer_sample_kernel,
        out_shape=(jax.ShapeDtypeStruct((nb, d), F32),
                   jax.ShapeDtypeStruct((nb, KV_WIDTH), F32),
                   jax.ShapeDtypeStruct((nb, KV_WIDTH), F32),
                   jax.ShapeDtypeStruct((nb, GM_WIDTH), F32)),
        in_specs=[pl.BlockSpec(memory_space=pltpu.SMEM)] + [vmem] * 15,
        out_specs=(vmem, vmem, vmem, vmem),
        scratch_shapes=[pltpu.VMEM((nb, ATTN_WIDTH), F32)],
        compiler_params=pltpu.CompilerParams(vmem_limit_bytes=VMEM_LIMIT_BYTES),
        name="mixer_sample",
    )(sinks, x, ck, cv, nmw, win, qnw, knw, gvw, cos, sin, bd, w00, b0, onw, wout)


def _rope_tables(pos):
    half = HEAD_DIM // 2
    inv = ROPE_THETA ** (-jnp.arange(half, dtype=F32) / half)
    ang = pos.astype(F32)[:, None] * inv[None, :]
    cos, sin = jnp.cos(ang), jnp.sin(ang)
    return (jnp.concatenate([cos, cos, cos, cos], axis=1),
            jnp.concatenate([-sin, sin, -sin, sin], axis=1))


PROMPT_BLOCK = 512
PEER_TOKENS = 512
PEER_EXPERT_TILE = 1024


def kernel(x_prompt, x_sample, cache_k, cache_v, norm_mix_w, w_in, q_norm_w, k_norm_w, sinks,
           gm_v_norm_w, w_spatial, b_spatial, out_norm_w, w_out, norm_ffn_w, w_query, sub_keys,
           expert_u, expert_v):
    depth, batch, seq = w_in.shape[0], x_prompt.shape[0], x_prompt.shape[1]
    dec_batch, dec_seq = x_sample.shape[0], x_sample.shape[1]
    assert depth == 1 and batch == 1 and dec_seq == 1
    past_len = seq
    layer = 0

    nmw = norm_mix_w[layer][None, :]
    win = w_in[layer].astype(BF16)
    qnw = jnp.tile(q_norm_w[layer], ATTN_HEADS)[None, :]
    knw = jnp.tile(k_norm_w[layer], KV_HEADS)[None, :]
    gvw = gm_v_norm_w[layer].reshape(1, GM_WIDTH)
    group = jnp.arange(ATTN_WIDTH) // HEAD_DIM
    bd = jnp.where(group[:, None] == group[None, :], 1.0 / HEAD_DIM, 0.0).astype(BF16)
    wsp = w_spatial[layer]
    bsp = jnp.repeat(b_spatial[layer].T, GM_HEAD_DIM, axis=1)
    w00 = jnp.repeat(w_spatial[layer][:, 0, 0], GM_HEAD_DIM)[None, :]
    b0 = jnp.repeat(b_spatial[layer][:, 0], GM_HEAD_DIM)[None, :]
    onw = out_norm_w[layer][None, :]
    wout = w_out[layer].astype(BF16)
    nfw = norm_ffn_w[layer][None, :]
    wqt = w_query[layer].T.astype(BF16)
    skeys = sub_keys[layer].reshape(PEER_HEADS * 2, N_KEYS, HALF_KEY).astype(BF16)
    u_bf = expert_u[layer].astype(BF16)
    vt_bf = expert_v[layer].astype(BF16).T
    sinks_l = sinks[layer]
    cos_p, sin_p = _rope_tables(jnp.arange(seq, dtype=jnp.int32))
    cos_s, sin_s = _rope_tables(past_len + jnp.zeros((dec_batch,), jnp.int32))

    h_p, nk_p, nv_p = _mixer_prompt_call(x_prompt[0], sinks_l, nmw, win, qnw, knw, gvw, cos_p, sin_p,
                                         bd, wsp, bsp, onw, wout, tb=PROMPT_BLOCK)
    y_p = _peer_call(h_p, nfw, wqt, skeys, u_bf, vt_bf, tn=PEER_TOKENS, et=PEER_EXPERT_TILE)

    ck = cache_k[layer].reshape(dec_batch, WINDOW, KV_WIDTH)
    cv = cache_v[layer].reshape(dec_batch, WINDOW, KV_WIDTH)
    h_s, k_s, v_s, gv_s = _mixer_sample_call(x_sample[:, 0], sinks_l, ck, cv, nmw, win, qnw, knw, gvw,
                                             cos_s, sin_s, bd, w00, b0, onw, wout)
    y_s = _peer_call(h_s, nfw, wqt, skeys, u_bf, vt_bf, tn=dec_batch, et=PEER_EXPERT_TILE)

    nk_s = jnp.concatenate([ck[:, 1:], k_s[:, None, :]], axis=1)
    nv_s = jnp.concatenate([cv[:, 1:], v_s[:, None, :]], axis=1)
    return (y_p[None],
            y_s[:, None, :],
            nk_p.reshape(1, 1, WINDOW, KV_HEADS, HEAD_DIM),
            nv_p.reshape(1, 1, WINDOW, KV_HEADS, HEAD_DIM),
            nk_s.reshape(1, dec_batch, WINDOW, KV_HEADS, HEAD_DIM),
            nv_s.reshape(1, dec_batch, WINDOW, KV_HEADS, HEAD_DIM),
            gv_s.reshape(1, dec_batch, 1, GM_HEADS, GM_HEAD_DIM))
```

```python
import functools

import jax
import jax.numpy as jnp
import numpy as np
from jax import lax
from jax.experimental import pallas as pl
from jax.experimental.pallas import tpu as pltpu

F32 = jnp.float32
BF16 = jnp.bfloat16

D_MODEL = 1024
HEAD_DIM = 64
ATTN_HEADS = 8
KV_HEADS = 2
GQA_GROUP = ATTN_HEADS // KV_HEADS
WINDOW = 128
ROPE_THETA = 10000.0
GM_HEADS = 8
GM_HEAD_DIM = 64
CHUNK = 128
ATTN_WIDTH = ATTN_HEADS * HEAD_DIM
KV_WIDTH = KV_HEADS * HEAD_DIM
GM_WIDTH = GM_HEADS * GM_HEAD_DIM
D_MIX = ATTN_WIDTH + GM_WIDTH
D_IN = ATTN_WIDTH + 2 * KV_WIDTH + 2 * GM_WIDTH
PEER_HEADS = 8
N_KEYS = 128
N_EXPERTS = N_KEYS * N_KEYS
D_KEY = 256
HALF_KEY = D_KEY // 2
TOPK = 16
EPS = 1e-6
NEG_INF = -1e30

LANES = 128
SUBLANES = 8
VMEM_LIMIT_BYTES = 56 * 1024 * 1024


def _gelu(x):
    return 0.5 * x * (1.0 + lax.erf(x * np.float32(np.sqrt(0.5))))


def _two_gelu(x):
    return x * (1.0 + lax.erf(x * np.float32(np.sqrt(0.5))))


def _rms_rows(x, w):
    ms = jnp.mean(x * x, axis=-1, keepdims=True)
    return x * lax.rsqrt(ms + EPS) * w


def _bitonic_merge_desc(xs):
    xs = list(xs)
    n = len(xs)
    j = n // 2
    while j >= 1:
        for i in range(n):
            l = i ^ j
            if l > i:
                a, b = xs[i], xs[l]
                xs[i] = jnp.maximum(a, b)
                xs[l] = jnp.minimum(a, b)
        j //= 2
    return xs


def _bitonic_sort_desc(xs):
    xs = list(xs)
    n = len(xs)
    k = 2
    while k <= n:
        j = k // 2
        while j >= 1:
            for i in range(n):
                l = i ^ j
                if l > i:
                    a, b = xs[i], xs[l]
                    hi, lo = jnp.maximum(a, b), jnp.minimum(a, b)
                    if (i & k) == 0:
                        xs[i], xs[l] = hi, lo
                    else:
                        xs[i], xs[l] = lo, hi
            j //= 2
        k *= 2
    return xs


def _top16_rows_desc(s):
    xs = _bitonic_sort_desc([s[SUBLANES * r:SUBLANES * (r + 1), :] for r in range(TOPK)])
    for shift in (4, 2, 1):
        other = [pltpu.roll(x, shift, axis=0) for x in xs]
        xs = _bitonic_merge_desc([jnp.maximum(xs[i], other[TOPK - 1 - i]) for i in range(TOPK)])
    return xs


_CAND_PAIRS = [(i, j) for i in range(TOPK) for j in range(TOPK) if (i + 1) * (j + 1) <= TOPK]


def _peer_kernel(h_ref, nw_ref, wqt_ref, sk_ref, u_ref, vt_ref, y_ref,
                 hnT_ref, qT_ref, s1_ref, s2_ref, e1_ref, se_ref, thr_ref, rows_ref, a_ref, g_ref,
                 acc_ref,
                 *, tn, et, strip):
    j = pl.program_id(1)
    n_sub = et // N_KEYS
    n_lt = tn // LANES

    @pl.when(j == 0)
    def _prologue():
        hn = _rms_rows(h_ref[...], nw_ref[...])
        hnT = hn.T.astype(BF16)
        hnT_ref[...] = hnT
        qT_ref[...] = jnp.dot(wqt_ref[...], hnT, preferred_element_type=F32)
        sub = lax.broadcasted_iota(jnp.int32, (SUBLANES, tn), 0)

        def head_body(hd, tops):
            tops = list(tops)
            for p, s_ref in enumerate((s1_ref, s2_ref)):
                c0 = pl.multiple_of((hd * 2 + p) * HALF_KEY, HALF_KEY)
                q = qT_ref[pl.ds(c0, HALF_KEY), :].astype(BF16)
                s = jnp.dot(sk_ref[hd * 2 + p], q, preferred_element_type=F32)
                s_ref[hd] = s
                ranked = _top16_rows_desc(s)
                for r in range(TOPK):
                    tops[p * TOPK + r] = jnp.where(sub == hd, ranked[r], tops[p * TOPK + r])
            return tuple(tops)

        tops = lax.fori_loop(0, PEER_HEADS, head_body,
                             tuple(jnp.zeros((SUBLANES, tn), F32) for _ in range(2 * TOPK)))
        top1, top2 = tops[:TOPK], tops[TOPK:]
        cands = [top1[i] + top2[jj] for (i, jj) in _CAND_PAIRS]
        cands += [jnp.full((SUBLANES, tn), -jnp.inf, F32)] * (64 - len(cands))
        best = _bitonic_sort_desc(cands)[:TOPK]
        m = best[0]
        z = jnp.exp(best[0] - m)
        for r in range(1, TOPK):
            z = z + jnp.exp(best[r] - m)
        thr = best[TOPK - 1]
        inv_z = 1.0 / z
        n_rg = N_KEYS // SUBLANES
        for hd in range(PEER_HEADS):
            e1_ref[hd] = jnp.exp(s1_ref[hd] - top1[0][hd:hd + 1, :])
            s2 = s2_ref[hd]
            e2 = jnp.exp(s2 - top2[0][hd:hd + 1, :]) * (0.5 * inv_z[hd:hd + 1, :])
            for t in range(n_lt):
                lanes = slice(t * LANES, (t + 1) * LANES)
                se_ref[hd, t, :, 0] = s2[:, lanes].reshape(n_rg, SUBLANES, LANES)
                se_ref[hd, t, :, 1] = e2[:, lanes].reshape(n_rg, SUBLANES, LANES)
                thr_ref[t, hd] = jnp.broadcast_to(thr[hd:hd + 1, lanes], (SUBLANES, LANES))
        acc_ref[...] = jnp.zeros_like(acc_ref)

    a_ref[...] = jnp.dot(u_ref[...], hnT_ref[...], preferred_element_type=F32)

    i1_base = pl.multiple_of(j * n_sub, SUBLANES)
    for r in range(n_sub):
        for hd in range(PEER_HEADS):
            g8 = pl.ds(i1_base + (r // SUBLANES) * SUBLANES, SUBLANES)
            o = r % SUBLANES
            s1g = s1_ref[hd, g8, :]
            e1g = e1_ref[hd, g8, :]
            for t in range(n_lt):
                lanes = slice(t * LANES, (t + 1) * LANES)
                rows_ref[t, hd, 0] = jnp.broadcast_to(s1g[o:o + 1, lanes], (SUBLANES, LANES))
                rows_ref[t, hd, 1] = jnp.broadcast_to(e1g[o:o + 1, lanes], (SUBLANES, LANES))

        def strip_body(sidx, carry, r=r):
            row0 = sidx * strip
            erow = r * N_KEYS + sidx * strip
            for t in range(n_lt):
                lanes = slice(t * LANES, (t + 1) * LANES)
                ws = [None] * (strip // SUBLANES)
                for hd in range(PEER_HEADS):
                    s1row = rows_ref[t, hd, 0]
                    e1row = rows_ref[t, hd, 1]
                    th = thr_ref[t, hd]
                    for v in range(strip // SUBLANES):
                        rg = row0 // SUBLANES + v
                        s2 = se_ref[hd, t, rg, 0]
                        e2 = se_ref[hd, t, rg, 1]
                        term = jnp.where(s2 + s1row >= th, e2 * e1row, 0.0)
                        ws[v] = term if ws[v] is None else ws[v] + term
                w = jnp.concatenate(ws, axis=0)
                a = a_ref[pl.ds(erow, strip), lanes]
                g_ref[pl.ds(erow, strip), lanes] = (w * _two_gelu(a)).astype(BF16)
            return carry

        for sidx in range(N_KEYS // strip):
            strip_body(sidx, 0)

    acc_ref[...] += jnp.dot(vt_ref[...], g_ref[...], preferred_element_type=F32)

    @pl.when(j == pl.num_programs(1) - 1)
    def _epilogue():
        y_ref[...] = h_ref[...] + acc_ref[...].T


def _peer_call(h, norm_w, wqt, skeys, u_bf, vt_bf, *, tn, et, strip=32):
    n, d = h.shape
    n_exp = u_bf.shape[0]
    assert n % tn == 0 and n_exp % et == 0 and et % (SUBLANES * N_KEYS) == 0 and tn % LANES == 0
    grid = (n // tn, n_exp // et)
    kern = functools.partial(_peer_kernel, tn=tn, et=et, strip=strip)
    table = pltpu.VMEM((PEER_HEADS, N_KEYS, tn), F32)
    return pl.pallas_call(
        kern,
        out_shape=jax.ShapeDtypeStruct((n, d), F32),
        grid=grid,
        in_specs=[
            pl.BlockSpec((tn, d), lambda i, j: (i, 0)),
            pl.BlockSpec((1, d), lambda i, j: (0, 0)),
            pl.BlockSpec(wqt.shape, lambda i, j: (0, 0)),
            pl.BlockSpec(skeys.shape, lambda i, j: (0, 0, 0)),
            pl.BlockSpec((et, d), lambda i, j: (j, 0)),
            pl.BlockSpec((d, et), lambda i, j: (0, j)),
        ],
        out_specs=pl.BlockSpec((tn, d), lambda i, j: (i, 0)),
        scratch_shapes=[
            pltpu.VMEM((d, tn), BF16),
            pltpu.VMEM((wqt.shape[0], tn), F32),
            table, table, table,
            pltpu.VMEM((PEER_HEADS, tn // LANES, N_KEYS // SUBLANES, 2, SUBLANES, LANES), F32),
            pltpu.VMEM((tn // LANES, PEER_HEADS, SUBLANES, LANES), F32),
            pltpu.VMEM((tn // LANES, PEER_HEADS, 2, SUBLANES, LANES), F32),
            pltpu.VMEM((et, tn), F32),
            pltpu.VMEM((et, tn), BF16),
            pltpu.VMEM((d, tn), F32),
        ],
        compiler_params=pltpu.CompilerParams(
            dimension_semantics=("arbitrary", "arbitrary"),
            vmem_limit_bytes=VMEM_LIMIT_BYTES),
        name="peer",
    )(h, norm_w, wqt, skeys, u_bf, vt_bf)


def _group_mean_sq(x, bd):
    x2 = x * x
    hi = x2.astype(BF16)
    lo = (x2 - hi.astype(F32)).astype(BF16)
    return (jnp.dot(hi, bd, preferred_element_type=F32) + jnp.dot(lo, bd, preferred_element_type=F32))


def _head_norm(x, w, bd):
    return x * lax.rsqrt(_group_mean_sq(x, bd) + EPS) * w


def _rope(x, cos, sin_signed):
    width = x.shape[-1]
    half = HEAD_DIM // 2
    lane = lax.broadcasted_iota(jnp.int32, x.shape, 1)
    first_half = (lane % HEAD_DIM) < half
    rot = jnp.where(first_half, pltpu.roll(x, width - half, axis=1), pltpu.roll(x, half, axis=1))
    return x * cos + rot * sin_signed


def _project_in(x, nmw, win, qnw, knw, gvw, cos, sin_signed, bd):
    xn = _rms_rows(x, nmw).astype(BF16)
    z = jnp.dot(xn, win, preferred_element_type=F32)
    o1, o2, o3, o4 = ATTN_WIDTH, ATTN_WIDTH + KV_WIDTH, ATTN_WIDTH + 2 * KV_WIDTH, D_IN - GM_WIDTH
    reps = ATTN_WIDTH // LANES
    cos_q = jnp.concatenate([cos] * reps, axis=1)
    sin_q = jnp.concatenate([sin_signed] * reps, axis=1)
    q = _rope(_head_norm(z[:, :o1], qnw, bd), cos_q, sin_q) * (HEAD_DIM ** -0.5)
    k = _rope(_head_norm(z[:, o1:o2], knw, bd[:KV_WIDTH, :KV_WIDTH]), cos, sin_signed)
    v = z[:, o2:o3]
    ug = _gelu(z[:, o3:o4])
    gvn = _head_norm(_gelu(z[:, o4:]), gvw, bd)
    return q, k, v, ug, gvn


def _sink_softmax_pv(s, sink, v_bf):
    m = jnp.maximum(jnp.max(s, axis=-1, keepdims=True), sink)
    p = jnp.exp(s - m)
    denom = jnp.sum(p, axis=-1, keepdims=True) + jnp.exp(sink - m)
    return jnp.dot((p / denom).astype(BF16), v_bf, preferred_element_type=F32)


def _merge(x, attn, gm, onw, wout):
    a = _rms_rows(attn, onw[:, :ATTN_WIDTH])
    g = _rms_rows(gm, onw[:, ATTN_WIDTH:])
    cat = jnp.concatenate([a, g], axis=1).astype(BF16)
    return x + jnp.dot(cat, wout, preferred_element_type=F32)


def _mixer_prompt_kernel(sinks_ref, x_ref, nmw_ref, win_ref, qnw_ref, knw_ref, gvw_ref, cos_ref,
                         sin_ref, bd_ref, wsp_ref, bsp_ref, onw_ref, wout_ref,
                         h_ref, kout_ref, vout_ref,
                         q_ref, kext_ref, vext_ref, ug_ref, gvn_ref, attn_ref, gm_ref, wtril_ref,
                         *, tb):
    i = pl.program_id(0)
    nsb = tb // WINDOW

    @pl.when(i == 0)
    def _init():
        kext_ref[0:WINDOW, :] = jnp.zeros((WINDOW, KV_WIDTH), BF16)
        vext_ref[0:WINDOW, :] = jnp.zeros((WINDOW, KV_WIDTH), BF16)
        row = lax.broadcasted_iota(jnp.int32, (CHUNK, CHUNK), 0)
        col = lax.broadcasted_iota(jnp.int32, (CHUNK, CHUNK), 1)
        for hd in range(GM_HEADS):
            wtril_ref[hd] = jnp.where(row >= col, wsp_ref[hd], 0.0).astype(BF16)

    x = x_ref[...]
    q, k, v, ug, gvn = _project_in(x, nmw_ref[...], win_ref[...], qnw_ref[...], knw_ref[...],
                                   gvw_ref[...], cos_ref[...], sin_ref[...], bd_ref[...])
    q_ref[...] = q.astype(BF16)
    kext_ref[WINDOW:, :] = k.astype(BF16)
    vext_ref[WINDOW:, :] = v.astype(BF16)
    ug_ref[...] = ug
    gvn_ref[...] = gvn.astype(BF16)
    kout_ref[...] = k[tb - WINDOW:, :]
    vout_ref[...] = v[tb - WINDOW:, :]

    qi = lax.broadcasted_iota(jnp.int32, (WINDOW, 2 * WINDOW), 0)
    kj = lax.broadcasted_iota(jnp.int32, (WINDOW, 2 * WINDOW), 1)
    band = (kj > qi) & (kj <= qi + WINDOW)
    lane = lax.broadcasted_iota(jnp.int32, (CHUNK, LANES), 1)

    def sub_block(b, carry):
        r0 = pl.multiple_of(b * WINDOW, WINDOW)
        rows = pl.ds(r0, WINDOW)
        kb = kext_ref[pl.ds(r0, 2 * WINDOW), :]
        vb = vext_ref[pl.ds(r0, 2 * WINDOW), :]
        first = jnp.logical_and(i == 0, b == 0)
        mask = band & (kj >= jnp.where(first, WINDOW, 0))
        qb = q_ref[rows, :]
        outs = []
        for hd in range(ATTN_HEADS):
            g = hd // GQA_GROUP
            qh = qb[:, hd * HEAD_DIM:(hd + 1) * HEAD_DIM]
            kg = kb[:, g * HEAD_DIM:(g + 1) * HEAD_DIM]
            vg = vb[:, g * HEAD_DIM:(g + 1) * HEAD_DIM]
            s = lax.dot_general(qh, kg, (((1,), (1,)), ((), ())), preferred_element_type=F32)
            s = jnp.where(mask, s, NEG_INF)
            outs.append(_sink_softmax_pv(s, sinks_ref[hd], vg))
        attn_ref[rows, :] = jnp.concatenate(outs, axis=1)
        gvb = gvn_ref[rows, :]
        tiles = []
        for t in range(GM_WIDTH // LANES):
            gt = gvb[:, t * LANES:(t + 1) * LANES]
            y0 = jnp.dot(wtril_ref[2 * t], gt, preferred_element_type=F32)
            y1 = jnp.dot(wtril_ref[2 * t + 1], gt, preferred_element_type=F32)
            tiles.append(jnp.where(lane < GM_HEAD_DIM, y0, y1))
        mixed = jnp.concatenate(tiles, axis=1) + bsp_ref[...]
        gm_ref[rows, :] = ug_ref[rows, :] * mixed
        return carry

    lax.fori_loop(0, nsb, sub_block, 0)

    h_ref[...] = _merge(x, attn_ref[...], gm_ref[...], onw_ref[...], wout_ref[...])
    kext_ref[0:WINDOW, :] = kext_ref[tb:tb + WINDOW, :]
    vext_ref[0:WINDOW, :] = vext_ref[tb:tb + WINDOW, :]


def _const_spec(shape):
    nd = len(shape)
    return pl.BlockSpec(shape, lambda i: (0,) * nd)


def _mixer_prompt_call(x, sinks, nmw, win, qnw, knw, gvw, cos, sin, bd, wsp, bsp, onw, wout, *, tb):
    n, d = x.shape
    assert n % tb == 0 and tb % WINDOW == 0
    kern = functools.partial(_mixer_prompt_kernel, tb=tb)
    return pl.pallas_call(
        kern,
        out_shape=(jax.ShapeDtypeStruct((n, d), F32),
                   jax.ShapeDtypeStruct((WINDOW, KV_WIDTH), F32),
                   jax.ShapeDtypeStruct((WINDOW, KV_WIDTH), F32)),
        grid=(n // tb,),
        in_specs=[
            pl.BlockSpec(memory_space=pltpu.SMEM),
            pl.BlockSpec((tb, d), lambda i: (i, 0)),
            _const_spec(nmw.shape), _const_spec(win.shape), _const_spec(qnw.shape),
            _const_spec(knw.shape), _const_spec(gvw.shape),
            pl.BlockSpec((tb, LANES), lambda i: (i, 0)),
            pl.BlockSpec((tb, LANES), lambda i: (i, 0)),
            _const_spec(bd.shape), _const_spec(wsp.shape), _const_spec(bsp.shape),
            _const_spec(onw.shape), _const_spec(wout.shape),
        ],
        out_specs=(pl.BlockSpec((tb, d), lambda i: (i, 0)),
                   _const_spec((WINDOW, KV_WIDTH)), _const_spec((WINDOW, KV_WIDTH))),
        scratch_shapes=[
            pltpu.VMEM((tb, ATTN_WIDTH), BF16),
            pltpu.VMEM((tb + WINDOW, KV_WIDTH), BF16),
            pltpu.VMEM((tb + WINDOW, KV_WIDTH), BF16),
            pltpu.VMEM((tb, GM_WIDTH), F32),
            pltpu.VMEM((tb, GM_WIDTH), BF16),
            pltpu.VMEM((tb, ATTN_WIDTH), F32),
            pltpu.VMEM((tb, GM_WIDTH), F32),
            pltpu.VMEM((GM_HEADS, CHUNK, CHUNK), BF16),
        ],
        compiler_params=pltpu.CompilerParams(
            dimension_semantics=("arbitrary",), vmem_limit_bytes=VMEM_LIMIT_BYTES),
        name="mixer_prompt",
    )(sinks, x, nmw, win, qnw, knw, gvw, cos, sin, bd, wsp, bsp, onw, wout)


SAMPLE_CHUNK = 8


def _mixer_sample_kernel(sinks_ref, x_ref, ck_ref, cv_ref, nmw_ref, win_ref, qnw_ref, knw_ref,
                         gvw_ref, cos_ref, sin_ref, bd_ref, w00_ref, b0_ref, onw_ref, wout_ref,
                         h_ref, kout_ref, vout_ref, gvout_ref, attn_ref):
    nb = x_ref.shape[0]
    x = x_ref[...]
    q, k, v, ug, gvn = _project_in(x, nmw_ref[...], win_ref[...], qnw_ref[...], knw_ref[...],
                                   gvw_ref[...], cos_ref[...], sin_ref[...], bd_ref[...])
    kout_ref[...] = k
    vout_ref[...] = v
    gvout_ref[...] = gvn

    cb = SAMPLE_CHUNK
    lane = lax.broadcasted_iota(jnp.int32, (cb, LANES), 1)
    jrow = lax.broadcasted_iota(jnp.int32, (WINDOW, LANES), 0)
    rseq = lax.broadcasted_iota(jnp.int32, (GQA_GROUP * cb, cb * WINDOW), 0) % cb
    cseq = lax.broadcasted_iota(jnp.int32, (GQA_GROUP * cb, cb * WINDOW), 1) // WINDOW
    same_seq = rseq == cseq
    for c in range(nb // cb):
        rows = slice(c * cb, (c + 1) * cb)
        k_eff = jnp.concatenate(
            [jnp.where(jrow == 0, k[b:b + 1, :], ck_ref[b]) for b in range(c * cb, (c + 1) * cb)],
            axis=0).astype(BF16)
        v_eff = jnp.concatenate(
            [jnp.where(jrow == 0, v[b:b + 1, :], cv_ref[b]) for b in range(c * cb, (c + 1) * cb)],
            axis=0).astype(BF16)
        outs = [None] * ATTN_HEADS
        for g in range(KV_HEADS):
            qz = []
            for hh in range(GQA_GROUP):
                hd = g * GQA_GROUP + hh
                tile = q[rows, (hd // 2) * LANES:(hd // 2 + 1) * LANES]
                if hd % 2 != g:
                    tile = pltpu.roll(tile, HEAD_DIM, axis=1)
                qz.append(jnp.where((lane // HEAD_DIM) == g, tile, 0.0))
            qz = jnp.concatenate(qz, axis=0).astype(BF16)
            s = lax.dot_general(qz, k_eff, (((1,), (1,)), ((), ())), preferred_element_type=F32)
            s = jnp.where(same_seq, s, NEG_INF)
            for hh in range(GQA_GROUP):
                hd = g * GQA_GROUP + hh
                o = _sink_softmax_pv(s[hh * cb:(hh + 1) * cb, :], sinks_ref[hd], v_eff)
                outs[hd] = o[:, g * HEAD_DIM:(g + 1) * HEAD_DIM]
        attn_ref[rows, :] = jnp.concatenate(outs, axis=1)

    gm = ug * (w00_ref[...] * gvn + b0_ref[...])
    h_ref[...] = _merge(x, attn_ref[...], gm, onw_ref[...], wout_ref[...])


def _mixer_sample_call(x, sinks, ck, cv, nmw, win, qnw, knw, gvw, cos, sin, bd, w00, b0, onw, wout):
    nb, d = x.shape
    assert nb % SAMPLE_CHUNK == 0
    vmem = pl.BlockSpec(memory_space=pltpu.VMEM)
    return pl.pallas_call(
        _mixer_sample_kernel,
        out_shape=(jax.ShapeDtypeStruct((nb, d), F32),
                   jax.ShapeDtypeStruct((nb, KV_WIDTH), F32),
                   jax.ShapeDtypeStruct((nb, KV_WIDTH), F32),
                   jax.ShapeDtypeStruct((nb, GM_WIDTH), F32)),
        in_specs=[pl.BlockSpec(memory_space=pltpu.SMEM)] + [vmem] * 15,
        out_specs=(vmem, vmem, vmem, vmem),
        scratch_shapes=[pltpu.VMEM((nb, ATTN_WIDTH), F32)],
        compiler_params=pltpu.CompilerParams(vmem_limit_bytes=VMEM_LIMIT_BYTES),
        name="mixer_sample",
    )(sinks, x, ck, cv, nmw, win, qnw, knw, gvw, cos, sin, bd, w00, b0, onw, wout)


def _rope_tables(pos):
    half = HEAD_DIM // 2
    inv = ROPE_THETA ** (-jnp.arange(half, dtype=F32) / half)
    ang = pos.astype(F32)[:, None] * inv[None, :]
    cos, sin = jnp.cos(ang), jnp.sin(ang)
    return (jnp.concatenate([cos, cos, cos, cos], axis=1),
            jnp.concatenate([-sin, sin, -sin, sin], axis=1))


PROMPT_BLOCK = 512
PEER_TOKENS = 512
PEER_EXPERT_TILE = 1024


def kernel(x_prompt, x_sample, cache_k, cache_v, norm_mix_w, w_in, q_norm_w, k_norm_w, sinks,
           gm_v_norm_w, w_spatial, b_spatial, out_norm_w, w_out, norm_ffn_w, w_query, sub_keys,
           expert_u, expert_v):
    depth, batch, seq = w_in.shape[0], x_prompt.shape[0], x_prompt.shape[1]
    dec_batch, dec_seq = x_sample.shape[0], x_sample.shape[1]
    assert depth == 1 and batch == 1 and dec_seq == 1
    past_len = seq
    layer = 0

    nmw = norm_mix_w[layer][None, :]
    win = w_in[layer].astype(BF16)
    qnw = jnp.tile(q_norm_w[layer], ATTN_HEADS)[None, :]
    knw = jnp.tile(k_norm_w[layer], KV_HEADS)[None, :]
    gvw = gm_v_norm_w[layer].reshape(1, GM_WIDTH)
    group = jnp.arange(ATTN_WIDTH) // HEAD_DIM
    bd = jnp.where(group[:, None] == group[None, :], 1.0 / HEAD_DIM, 0.0).astype(BF16)
    wsp = w_spatial[layer]
    bsp = jnp.repeat(b_spatial[layer].T, GM_HEAD_DIM, axis=1)
    w00 = jnp.repeat(w_spatial[layer][:, 0, 0], GM_HEAD_DIM)[None, :]
    b0 = jnp.repeat(b_spatial[layer][:, 0], GM_HEAD_DIM)[None, :]
    onw = out_norm_w[layer][None, :]
    wout = w_out[layer].astype(BF16)
    nfw = norm_ffn_w[layer][None, :]
    wqt = w_query[layer].T.astype(BF16)
    skeys = sub_keys[layer].reshape(PEER_HEADS * 2, N_KEYS, HALF_KEY).astype(BF16)
    u_bf = expert_u[layer].astype(BF16)
    vt_bf = expert_v[layer].T.astype(BF16)
    sinks_l = sinks[layer]
    cos_p, sin_p = _rope_tables(jnp.arange(seq, dtype=jnp.int32))
    cos_s, sin_s = _rope_tables(past_len + jnp.zeros((dec_batch,), jnp.int32))

    h_p, nk_p, nv_p = _mixer_prompt_call(x_prompt[0], sinks_l, nmw, win, qnw, knw, gvw, cos_p, sin_p,
                                         bd, wsp, bsp, onw, wout, tb=PROMPT_BLOCK)
    y_p = _peer_call(h_p, nfw, wqt, skeys, u_bf, vt_bf, tn=PEER_TOKENS, et=PEER_EXPERT_TILE)

    ck = cache_k[layer].reshape(dec_batch, WINDOW, KV_WIDTH)
    cv = cache_v[layer].reshape(dec_batch, WINDOW, KV_WIDTH)
    h_s, k_s, v_s, gv_s = _mixer_sample_call(x_sample[:, 0], sinks_l, ck, cv, nmw, win, qnw, knw, gvw,
                                             cos_s, sin_s, bd, w00, b0, onw, wout)
    y_s = _peer_call(h_s, nfw, wqt, skeys, u_bf, vt_bf, tn=dec_batch, et=PEER_EXPERT_TILE)

    nk_s = jnp.concatenate([ck[:, 1:], k_s[:, None, :]], axis=1)
    nv_s = jnp.concatenate([cv[:, 1:], v_s[:, None, :]], axis=1)
    return (y_p[None],
            y_s[:, None, :],
            nk_p.reshape(1, 1, WINDOW, KV_HEADS, HEAD_DIM),
            nv_p.reshape(1, 1, WINDOW, KV_HEADS, HEAD_DIM),
            nk_s.reshape(1, dec_batch, WINDOW, KV_HEADS, HEAD_DIM),
            nv_s.reshape(1, dec_batch, WINDOW, KV_HEADS, HEAD_DIM),
            gv_s.reshape(1, dec_batch, 1, GM_HEADS, GM_HEAD_DIM))
```

```python
import functools

import jax
import jax.numpy as jnp
import numpy as np
from jax import lax
from jax.experimental import pallas as pl
from jax.experimental.pallas import tpu as pltpu

F32 = jnp.float32
BF16 = jnp.bfloat16

D_MODEL = 1024
HEAD_DIM = 64
ATTN_HEADS = 8
KV_HEADS = 2
GQA_GROUP = ATTN_HEADS // KV_HEADS
WINDOW = 128
ROPE_THETA = 10000.0
GM_HEADS = 8
GM_HEAD_DIM = 64
CHUNK = 128
ATTN_WIDTH = ATTN_HEADS * HEAD_DIM
KV_WIDTH = KV_HEADS * HEAD_DIM
GM_WIDTH = GM_HEADS * GM_HEAD_DIM
D_MIX = ATTN_WIDTH + GM_WIDTH
D_IN = ATTN_WIDTH + 2 * KV_WIDTH + 2 * GM_WIDTH
PEER_HEADS = 8
N_KEYS = 128
N_EXPERTS = N_KEYS * N_KEYS
D_KEY = 256
HALF_KEY = D_KEY // 2
TOPK = 16
EPS = 1e-6
NEG_INF = -1e30

LANES = 128
SUBLANES = 8
VMEM_LIMIT_BYTES = 56 * 1024 * 1024


def _gelu(x):
    return 0.5 * x * (1.0 + lax.erf(x * np.float32(np.sqrt(0.5))))


def _two_gelu(x):
    return x * (1.0 + lax.erf(x * np.float32(np.sqrt(0.5))))


def _rms_rows(x, w):
    ms = jnp.mean(x * x, axis=-1, keepdims=True)
    return x * lax.rsqrt(ms + EPS) * w


def _bitonic_merge_desc(xs):
    xs = list(xs)
    n = len(xs)
    j = n // 2
    while j >= 1:
        for i in range(n):
            l = i ^ j
            if l > i:
                a, b = xs[i], xs[l]
                xs[i] = jnp.maximum(a, b)
                xs[l] = jnp.minimum(a, b)
        j //= 2
    return xs


def _bitonic_sort_desc(xs):
    xs = list(xs)
    n = len(xs)
    k = 2
    while k <= n:
        j = k // 2
        while j >= 1:
            for i in range(n):
                l = i ^ j
                if l > i:
                    a, b = xs[i], xs[l]
                    hi, lo = jnp.maximum(a, b), jnp.minimum(a, b)
                    if (i & k) == 0:
                        xs[i], xs[l] = hi, lo
                    else:
                        xs[i], xs[l] = lo, hi
            j //= 2
        k *= 2
    return xs


def _top16_rows_desc(s):
    xs = _bitonic_sort_desc([s[SUBLANES * r:SUBLANES * (r + 1), :] for r in range(TOPK)])
    for shift in (4, 2, 1):
        other = [pltpu.roll(x, shift, axis=0) for x in xs]
        xs = _bitonic_merge_desc([jnp.maximum(xs[i], other[TOPK - 1 - i]) for i in range(TOPK)])
    return xs


_CAND_PAIRS = [(i, j) for i in range(TOPK) for j in range(TOPK) if (i + 1) * (j + 1) <= TOPK]


def _peer_kernel(h_ref, nw_ref, wqt_ref, sk_ref, u_ref, vt_ref, y_ref,
                 hnT_ref, qT_ref, s1_ref, s2_ref, e1_ref, se_ref, thr_ref, rows_ref, a_ref, g_ref,
                 acc_ref,
                 *, tn, et, strip):
    j = pl.program_id(1)
    n_sub = et // N_KEYS
    n_lt = tn // LANES

    @pl.when(j == 0)
    def _prologue():
        hn = _rms_rows(h_ref[...], nw_ref[...])
        hnT = hn.T.astype(BF16)
        hnT_ref[...] = hnT
        qT_ref[...] = jnp.dot(wqt_ref[...], hnT, preferred_element_type=F32)
        sub = lax.broadcasted_iota(jnp.int32, (SUBLANES, tn), 0)

        def head_body(hd, tops):
            tops = list(tops)
            for p, s_ref in enumerate((s1_ref, s2_ref)):
                c0 = pl.multiple_of((hd * 2 + p) * HALF_KEY, HALF_KEY)
                q = qT_ref[pl.ds(c0, HALF_KEY), :].astype(BF16)
                s = jnp.dot(sk_ref[hd * 2 + p], q, preferred_element_type=F32)
                s_ref[hd] = s
                ranked = _top16_rows_desc(s)
                for r in range(TOPK):
                    tops[p * TOPK + r] = jnp.where(sub == hd, ranked[r], tops[p * TOPK + r])
            return tuple(tops)

        tops = lax.fori_loop(0, PEER_HEADS, head_body,
                             tuple(jnp.zeros((SUBLANES, tn), F32) for _ in range(2 * TOPK)))
        top1, top2 = tops[:TOPK], tops[TOPK:]
        cands = [top1[i] + top2[jj] for (i, jj) in _CAND_PAIRS]
        cands += [jnp.full((SUBLANES, tn), -jnp.inf, F32)] * (64 - len(cands))
        best = _bitonic_sort_desc(cands)[:TOPK]
        m = best[0]
        z = jnp.exp(best[0] - m)
        for r in range(1, TOPK):
            z = z + jnp.exp(best[r] - m)
        thr = best[TOPK - 1]
        inv_z = 1.0 / z
        n_rg = N_KEYS // SUBLANES
        for hd in range(PEER_HEADS):
            e1_ref[hd] = jnp.exp(s1_ref[hd] - top1[0][hd:hd + 1, :])
            s2 = s2_ref[hd]
            e2 = jnp.exp(s2 - top2[0][hd:hd + 1, :]) * (0.5 * inv_z[hd:hd + 1, :])
            for t in range(n_lt):
                lanes = slice(t * LANES, (t + 1) * LANES)
                se_ref[hd, t, :, 0] = s2[:, lanes].reshape(n_rg, SUBLANES, LANES)
                se_ref[hd, t, :, 1] = e2[:, lanes].reshape(n_rg, SUBLANES, LANES)
                thr_ref[t, hd] = jnp.broadcast_to(thr[hd:hd + 1, lanes], (SUBLANES, LANES))
        acc_ref[...] = jnp.zeros_like(acc_ref)

    a_ref[...] = jnp.dot(u_ref[...], hnT_ref[...], preferred_element_type=F32)

    i1_base = pl.multiple_of(j * n_sub, SUBLANES)
    for r in range(n_sub):
        for hd in range(PEER_HEADS):
            g8 = pl.ds(i1_base + (r // SUBLANES) * SUBLANES, SUBLANES)
            o = r % SUBLANES
            s1g = s1_ref[hd, g8, :]
            e1g = e1_ref[hd, g8, :]
            for t in range(n_lt):
                lanes = slice(t * LANES, (t + 1) * LANES)
                rows_ref[t, hd, 0] = jnp.broadcast_to(s1g[o:o + 1, lanes], (SUBLANES, LANES))
                rows_ref[t, hd, 1] = jnp.broadcast_to(e1g[o:o + 1, lanes], (SUBLANES, LANES))

        def strip_body(sidx, carry, r=r):
            row0 = sidx * strip
            erow = r * N_KEYS + sidx * strip
            for t in range(n_lt):
                lanes = slice(t * LANES, (t + 1) * LANES)
                ws = [None] * (strip // SUBLANES)
                for hd in range(PEER_HEADS):
                    s1row = rows_ref[t, hd, 0]
                    e1row = rows_ref[t, hd, 1]
                    th = thr_ref[t, hd]
                    for v in range(strip // SUBLANES):
                        rg = row0 // SUBLANES + v
                        s2 = se_ref[hd, t, rg, 0]
                        e2 = se_ref[hd, t, rg, 1]
                        term = jnp.where(s2 + s1row >= th, e2 * e1row, 0.0)
                        ws[v] = term if ws[v] is None else ws[v] + term
                w = jnp.concatenate(ws, axis=0)
                a = a_ref[pl.ds(erow, strip), lanes]
                g_ref[pl.ds(erow, strip), lanes] = (w * _two_gelu(a)).astype(BF16)
            return carry

        for sidx in range(N_KEYS // strip):
            strip_body(sidx, 0)

    acc_ref[...] += jnp.dot(vt_ref[...], g_ref[...], preferred_element_type=F32)

    @pl.when(j == pl.num_programs(1) - 1)
    def _epilogue():
        y_ref[...] = h_ref[...] + acc_ref[...].T


def _peer_call(h, norm_w, wqt, skeys, u_bf, vt_bf, *, tn, et, strip=64):
    n, d = h.shape
    n_exp = u_bf.shape[0]
    assert n % tn == 0 and n_exp % et == 0 and et % (SUBLANES * N_KEYS) == 0 and tn % LANES == 0
    grid = (n // tn, n_exp // et)
    kern = functools.partial(_peer_kernel, tn=tn, et=et, strip=strip)
    table = pltpu.VMEM((PEER_HEADS, N_KEYS, tn), F32)
    return pl.pallas_call(
        kern,
        out_shape=jax.ShapeDtypeStruct((n, d), F32),
        grid=grid,
        in_specs=[
            pl.BlockSpec((tn, d), lambda i, j: (i, 0)),
            pl.BlockSpec((1, d), lambda i, j: (0, 0)),
            pl.BlockSpec(wqt.shape, lambda i, j: (0, 0)),
            pl.BlockSpec(skeys.shape, lambda i, j: (0, 0, 0)),
            pl.BlockSpec((et, d), lambda i, j: (j, 0)),
            pl.BlockSpec((d, et), lambda i, j: (0, j)),
        ],
        out_specs=pl.BlockSpec((tn, d), lambda i, j: (i, 0)),
        scratch_shapes=[
            pltpu.VMEM((d, tn), BF16),
            pltpu.VMEM((wqt.shape[0], tn), F32),
            table, table, table,
            pltpu.VMEM((PEER_HEADS, tn // LANES, N_KEYS // SUBLANES, 2, SUBLANES, LANES), F32),
            pltpu.VMEM((tn // LANES, PEER_HEADS, SUBLANES, LANES), F32),
            pltpu.VMEM((tn // LANES, PEER_HEADS, 2, SUBLANES, LANES), F32),
            pltpu.VMEM((et, tn), F32),
            pltpu.VMEM((et, tn), BF16),
            pltpu.VMEM((d, tn), F32),
        ],
        compiler_params=pltpu.CompilerParams(
            dimension_semantics=("arbitrary", "arbitrary"),
            vmem_limit_bytes=VMEM_LIMIT_BYTES),
        name="peer",
    )(h, norm_w, wqt, skeys, u_bf, vt_bf)


def _group_mean_sq(x, bd):
    x2 = x * x
    hi = x2.astype(BF16)
    lo = (x2 - hi.astype(F32)).astype(BF16)
    return (jnp.dot(hi, bd, preferred_element_type=F32) + jnp.dot(lo, bd, preferred_element_type=F32))


def _head_norm(x, w, bd):
    return x * lax.rsqrt(_group_mean_sq(x, bd) + EPS) * w


def _rope(x, cos, sin_signed):
    width = x.shape[-1]
    half = HEAD_DIM // 2
    lane = lax.broadcasted_iota(jnp.int32, x.shape, 1)
    first_half = (lane % HEAD_DIM) < half
    rot = jnp.where(first_half, pltpu.roll(x, width - half, axis=1), pltpu.roll(x, half, axis=1))
    return x * cos + rot * sin_signed


def _project_in(x, nmw, win, qnw, knw, gvw, cos, sin_signed, bd):
    xn = _rms_rows(x, nmw).astype(BF16)
    z = jnp.dot(xn, win, preferred_element_type=F32)
    o1, o2, o3, o4 = ATTN_WIDTH, ATTN_WIDTH + KV_WIDTH, ATTN_WIDTH + 2 * KV_WIDTH, D_IN - GM_WIDTH
    reps = ATTN_WIDTH // LANES
    cos_q = jnp.concatenate([cos] * reps, axis=1)
    sin_q = jnp.concatenate([sin_signed] * reps, axis=1)
    q = _rope(_head_norm(z[:, :o1], qnw, bd), cos_q, sin_q) * (HEAD_DIM ** -0.5)
    k = _rope(_head_norm(z[:, o1:o2], knw, bd[:KV_WIDTH, :KV_WIDTH]), cos, sin_signed)
    v = z[:, o2:o3]
    ug = _gelu(z[:, o3:o4])
    gvn = _head_norm(_gelu(z[:, o4:]), gvw, bd)
    return q, k, v, ug, gvn


def _sink_softmax_pv(s, sink, v_bf):
    m = jnp.maximum(jnp.max(s, axis=-1, keepdims=True), sink)
    p = jnp.exp(s - m)
    denom = jnp.sum(p, axis=-1, keepdims=True) + jnp.exp(sink - m)
    return jnp.dot((p / denom).astype(BF16), v_bf, preferred_element_type=F32)


def _merge(x, attn, gm, onw, wout):
    a = _rms_rows(attn, onw[:, :ATTN_WIDTH])
    g = _rms_rows(gm, onw[:, ATTN_WIDTH:])
    cat = jnp.concatenate([a, g], axis=1).astype(BF16)
    return x + jnp.dot(cat, wout, preferred_element_type=F32)


def _mixer_prompt_kernel(sinks_ref, x_ref, nmw_ref, win_ref, qnw_ref, knw_ref, gvw_ref, cos_ref,
                         sin_ref, bd_ref, wsp_ref, bsp_ref, onw_ref, wout_ref,
                         h_ref, kout_ref, vout_ref,
                         q_ref, kext_ref, vext_ref, ug_ref, gvn_ref, attn_ref, gm_ref, wtril_ref,
                         *, tb):
    i = pl.program_id(0)
    nsb = tb // WINDOW

    @pl.when(i == 0)
    def _init():
        kext_ref[0:WINDOW, :] = jnp.zeros((WINDOW, KV_WIDTH), BF16)
        vext_ref[0:WINDOW, :] = jnp.zeros((WINDOW, KV_WIDTH), BF16)
        row = lax.broadcasted_iota(jnp.int32, (CHUNK, CHUNK), 0)
        col = lax.broadcasted_iota(jnp.int32, (CHUNK, CHUNK), 1)
        for hd in range(GM_HEADS):
            wtril_ref[hd] = jnp.where(row >= col, wsp_ref[hd], 0.0).astype(BF16)

    x = x_ref[...]
    q, k, v, ug, gvn = _project_in(x, nmw_ref[...], win_ref[...], qnw_ref[...], knw_ref[...],
                                   gvw_ref[...], cos_ref[...], sin_ref[...], bd_ref[...])
    q_ref[...] = q.astype(BF16)
    kext_ref[WINDOW:, :] = k.astype(BF16)
    vext_ref[WINDOW:, :] = v.astype(BF16)
    ug_ref[...] = ug
    gvn_ref[...] = gvn.astype(BF16)
    kout_ref[...] = k[tb - WINDOW:, :]
    vout_ref[...] = v[tb - WINDOW:, :]

    qi = lax.broadcasted_iota(jnp.int32, (WINDOW, 2 * WINDOW), 0)
    kj = lax.broadcasted_iota(jnp.int32, (WINDOW, 2 * WINDOW), 1)
    band = (kj > qi) & (kj <= qi + WINDOW)
    lane = lax.broadcasted_iota(jnp.int32, (CHUNK, LANES), 1)

    def sub_block(b, carry):
        r0 = pl.multiple_of(b * WINDOW, WINDOW)
        rows = pl.ds(r0, WINDOW)
        kb = kext_ref[pl.ds(r0, 2 * WINDOW), :]
        vb = vext_ref[pl.ds(r0, 2 * WINDOW), :]
        first = jnp.logical_and(i == 0, b == 0)
        mask = band & (kj >= jnp.where(first, WINDOW, 0))
        qb = q_ref[rows, :]
        outs = []
        for hd in range(ATTN_HEADS):
            g = hd // GQA_GROUP
            qh = qb[:, hd * HEAD_DIM:(hd + 1) * HEAD_DIM]
            kg = kb[:, g * HEAD_DIM:(g + 1) * HEAD_DIM]
            vg = vb[:, g * HEAD_DIM:(g + 1) * HEAD_DIM]
            s = lax.dot_general(qh, kg, (((1,), (1,)), ((), ())), preferred_element_type=F32)
            s = jnp.where(mask, s, NEG_INF)
            outs.append(_sink_softmax_pv(s, sinks_ref[hd], vg))
        attn_ref[rows, :] = jnp.concatenate(outs, axis=1)
        gvb = gvn_ref[rows, :]
        tiles = []
        for t in range(GM_WIDTH // LANES):
            gt = gvb[:, t * LANES:(t + 1) * LANES]
            y0 = jnp.dot(wtril_ref[2 * t], gt, preferred_element_type=F32)
            y1 = jnp.dot(wtril_ref[2 * t + 1], gt, preferred_element_type=F32)
            tiles.append(jnp.where(lane < GM_HEAD_DIM, y0, y1))
        mixed = jnp.concatenate(tiles, axis=1) + bsp_ref[...]
        gm_ref[rows, :] = ug_ref[rows, :] * mixed
        return carry

    lax.fori_loop(0, nsb, sub_block, 0)

    h_ref[...] = _merge(x, attn_ref[...], gm_ref[...], onw_ref[...], wout_ref[...])
    kext_ref[0:WINDOW, :] = kext_ref[tb:tb + WINDOW, :]
    vext_ref[0:WINDOW, :] = vext_ref[tb:tb + WINDOW, :]


def _const_spec(shape):
    nd = len(shape)
    return pl.BlockSpec(shape, lambda i: (0,) * nd)


def _mixer_prompt_call(x, sinks, nmw, win, qnw, knw, gvw, cos, sin, bd, wsp, bsp, onw, wout, *, tb):
    n, d = x.shape
    assert n % tb == 0 and tb % WINDOW == 0
    kern = functools.partial(_mixer_prompt_kernel, tb=tb)
    return pl.pallas_call(
        kern,
        out_shape=(jax.ShapeDtypeStruct((n, d), F32),
                   jax.ShapeDtypeStruct((WINDOW, KV_WIDTH), F32),
                   jax.ShapeDtypeStruct((WINDOW, KV_WIDTH), F32)),
        grid=(n // tb,),
        in_specs=[
            pl.BlockSpec(memory_space=pltpu.SMEM),
            pl.BlockSpec((tb, d), lambda i: (i, 0)),
            _const_spec(nmw.shape), _const_spec(win.shape), _const_spec(qnw.shape),
            _const_spec(knw.shape), _const_spec(gvw.shape),
            pl.BlockSpec((tb, LANES), lambda i: (i, 0)),
            pl.BlockSpec((tb, LANES), lambda i: (i, 0)),
            _const_spec(bd.shape), _const_spec(wsp.shape), _const_spec(bsp.shape),
            _const_spec(onw.shape), _const_spec(wout.shape),
        ],
        out_specs=(pl.BlockSpec((tb, d), lambda i: (i, 0)),
                   _const_spec((WINDOW, KV_WIDTH)), _const_spec((WINDOW, KV_WIDTH))),
        scratch_shapes=[
            pltpu.VMEM((tb, ATTN_WIDTH), BF16),
            pltpu.VMEM((tb + WINDOW, KV_WIDTH), BF16),
            pltpu.VMEM((tb + WINDOW, KV_WIDTH), BF16),
            pltpu.VMEM((tb, GM_WIDTH), F32),
            pltpu.VMEM((tb, GM_WIDTH), BF16),
            pltpu.VMEM((tb, ATTN_WIDTH), F32),
            pltpu.VMEM((tb, GM_WIDTH), F32),
            pltpu.VMEM((GM_HEADS, CHUNK, CHUNK), BF16),
        ],
        compiler_params=pltpu.CompilerParams(
            dimension_semantics=("arbitrary",), vmem_limit_bytes=VMEM_LIMIT_BYTES),
        name="mixer_prompt",
    )(sinks, x, nmw, win, qnw, knw, gvw, cos, sin, bd, wsp, bsp, onw, wout)


SAMPLE_CHUNK = 8


def _mixer_sample_kernel(sinks_ref, x_ref, ck_ref, cv_ref, nmw_ref, win_ref, qnw_ref, knw_ref,
                         gvw_ref, cos_ref, sin_ref, bd_ref, w00_ref, b0_ref, onw_ref, wout_ref,
                         h_ref, kout_ref, vout_ref, gvout_ref, attn_ref):
    nb = x_ref.shape[0]
    x = x_ref[...]
    q, k, v, ug, gvn = _project_in(x, nmw_ref[...], win_ref[...], qnw_ref[...], knw_ref[...],
                                   gvw_ref[...], cos_ref[...], sin_ref[...], bd_ref[...])
    kout_ref[...] = k
    vout_ref[...] = v
    gvout_ref[...] = gvn

    cb = SAMPLE_CHUNK
    lane = lax.broadcasted_iota(jnp.int32, (cb, LANES), 1)
    jrow = lax.broadcasted_iota(jnp.int32, (WINDOW, LANES), 0)
    rseq = lax.broadcasted_iota(jnp.int32, (GQA_GROUP * cb, cb * WINDOW), 0) % cb
    cseq = lax.broadcasted_iota(jnp.int32, (GQA_GROUP * cb, cb * WINDOW), 1) // WINDOW
    same_seq = rseq == cseq
    for c in range(nb // cb):
        rows = slice(c * cb, (c + 1) * cb)
        k_eff = jnp.concatenate(
            [jnp.where(jrow == 0, k[b:b + 1, :], ck_ref[b]) for b in range(c * cb, (c + 1) * cb)],
            axis=0).astype(BF16)
        v_eff = jnp.concatenate(
            [jnp.where(jrow == 0, v[b:b + 1, :], cv_ref[b]) for b in range(c * cb, (c + 1) * cb)],
            axis=0).astype(BF16)
        outs = [None] * ATTN_HEADS
        for g in range(KV_HEADS):
            qz = []
            for hh in range(GQA_GROUP):
                hd = g * GQA_GROUP + hh
                tile = q[rows, (hd // 2) * LANES:(hd // 2 + 1) * LANES]
                if hd % 2 != g:
                    tile = pltpu.roll(tile, HEAD_DIM, axis=1)
                qz.append(jnp.where((lane // HEAD_DIM) == g, tile, 0.0))
            qz = jnp.concatenate(qz, axis=0).astype(BF16)
            s = lax.dot_general(qz, k_eff, (((1,), (1,)), ((), ())), preferred_element_type=F32)
            s = jnp.where(same_seq, s, NEG_INF)
            for hh in range(GQA_GROUP):
                hd = g * GQA_GROUP + hh
                o = _sink_softmax_pv(s[hh * cb:(hh + 1) * cb, :], sinks_ref[hd], v_eff)
                outs[hd] = o[:, g * HEAD_DIM:(g + 1) * HEAD_DIM]
        attn_ref[rows, :] = jnp.concatenate(outs, axis=1)

    gm = ug * (w00_ref[...] * gvn + b0_ref[...])
    h_ref[...] = _merge(x, attn_ref[...], gm, onw_ref[...], wout_ref[...])


def _mixer_sample_call(x, sinks, ck, cv, nmw, win, qnw, knw, gvw, cos, sin, bd, w00, b0, onw, wout):
    nb, d = x.shape
    assert nb % SAMPLE_CHUNK == 0
    vmem = pl.BlockSpec(memory_space=pltpu.VMEM)
    return pl.pallas_call(
        _mixer_sample_kernel,
        out_shape=(jax.ShapeDtypeStruct((nb, d), F32),
                   jax.ShapeDtypeStruct((nb, KV_WIDTH), F32),
                   jax.ShapeDtypeStruct((nb, KV_WIDTH), F32),
                   jax.ShapeDtypeStruct((nb, GM_WIDTH), F32)),
        in_specs=[pl.BlockSpec(memory_space=pltpu.SMEM)] + [vmem] * 15,
        out_specs=(vmem, vmem, vmem, vmem),
        scratch_shapes=[pltpu.VMEM((nb, ATTN_WIDTH), F32)],
        compiler_params=pltpu.CompilerParams(vmem_limit_bytes=VMEM_LIMIT_BYTES),
        name="mixer_sample",
    )(sinks, x, ck, cv, nmw, win, qnw, knw, gvw, cos, sin, bd, w00, b0, onw, wout)


def _rope_tables(pos):
    half = HEAD_DIM // 2
    inv = ROPE_THETA ** (-jnp.arange(half, dtype=F32) / half)
    ang = pos.astype(F32)[:, None] * inv[None, :]
    cos, sin = jnp.cos(ang), jnp.sin(ang)
    return (jnp.concatenate([cos, cos, cos, cos], axis=1),
            jnp.concatenate([-sin, sin, -sin, sin], axis=1))


PROMPT_BLOCK = 512
PEER_TOKENS = 512
PEER_EXPERT_TILE = 1024


def kernel(x_prompt, x_sample, cache_k, cache_v, norm_mix_w, w_in, q_norm_w, k_norm_w, sinks,
           gm_v_norm_w, w_spatial, b_spatial, out_norm_w, w_out, norm_ffn_w, w_query, sub_keys,
           expert_u, expert_v):
    depth, batch, seq = w_in.shape[0], x_prompt.shape[0], x_prompt.shape[1]
    dec_batch, dec_seq = x_sample.shape[0], x_sample.shape[1]
    assert depth == 1 and batch == 1 and dec_seq == 1
    past_len = seq
    layer = 0

    nmw = norm_mix_w[layer][None, :]
    win = w_in[layer].astype(BF16)
    qnw = jnp.tile(q_norm_w[layer], ATTN_HEADS)[None, :]
    knw = jnp.tile(k_norm_w[layer], KV_HEADS)[None, :]
    gvw = gm_v_norm_w[layer].reshape(1, GM_WIDTH)
    group = jnp.arange(ATTN_WIDTH) // HEAD_DIM
    bd = jnp.where(group[:, None] == group[None, :], 1.0 / HEAD_DIM, 0.0).astype(BF16)
    wsp = w_spatial[layer]
    bsp = jnp.repeat(b_spatial[layer].T, GM_HEAD_DIM, axis=1)
    w00 = jnp.repeat(w_spatial[layer][:, 0, 0], GM_HEAD_DIM)[None, :]
    b0 = jnp.repeat(b_spatial[layer][:, 0], GM_HEAD_DIM)[None, :]
    onw = out_norm_w[layer][None, :]
    wout = w_out[layer].astype(BF16)
    nfw = norm_ffn_w[layer][None, :]
    wqt = w_query[layer].T.astype(BF16)
    skeys = sub_keys[layer].reshape(PEER_HEADS * 2, N_KEYS, HALF_KEY).astype(BF16)
    u_bf = expert_u[layer].astype(BF16)
    vt_bf = expert_v[layer].T.astype(BF16)
    sinks_l = sinks[layer]
    cos_p, sin_p = _rope_tables(jnp.arange(seq, dtype=jnp.int32))
    cos_s, sin_s = _rope_tables(past_len + jnp.zeros((dec_batch,), jnp.int32))

    h_p, nk_p, nv_p = _mixer_prompt_call(x_prompt[0], sinks_l, nmw, win, qnw, knw, gvw, cos_p, sin_p,
                                         bd, wsp, bsp, onw, wout, tb=PROMPT_BLOCK)
    y_p = _peer_call(h_p, nfw, wqt, skeys, u_bf, vt_bf, tn=PEER_TOKENS, et=PEER_EXPERT_TILE)

    ck = cache_k[layer].reshape(dec_batch, WINDOW, KV_WIDTH)
    cv = cache_v[layer].reshape(dec_batch, WINDOW, KV_WIDTH)
    h_s, k_s, v_s, gv_s = _mixer_sample_call(x_sample[:, 0], sinks_l, ck, cv, nmw, win, qnw, knw, gvw,
                                             cos_s, sin_s, bd, w00, b0, onw, wout)
    y_s = _peer_call(h_s, nfw, wqt, skeys, u_bf, vt_bf, tn=dec_batch, et=PEER_EXPERT_TILE)

    nk_s = jnp.concatenate([ck[:, 1:], k_s[:, None, :]], axis=1)
    nv_s = jnp.concatenate([cv[:, 1:], v_s[:, None, :]], axis=1)
    return (y_p[None],
            y_s[:, None, :],
            nk_p.reshape(1, 1, WINDOW, KV_HEADS, HEAD_DIM),
            nv_p.reshape(1, 1, WINDOW, KV_HEADS, HEAD_DIM),
            nk_s.reshape(1, dec_batch, WINDOW, KV_HEADS, HEAD_DIM),
            nv_s.reshape(1, dec_batch, WINDOW, KV_HEADS, HEAD_DIM),
            gv_s.reshape(1, dec_batch, 1, GM_HEADS, GM_HEAD_DIM))
```
